```python
import math
import jax, jax.numpy as jnp
from jax import lax
import numpy as np

D_MODEL = 1024
BATCH = 8
SEQ = 4096
DEPTH = 2

N_META = 16
HEAD_DIM = 64
D_MIX = D_MODEL
GROUP_W = D_MIX // 4
N_HEADS = GROUP_W // HEAD_DIM
D_FF = 4 * D_MODEL
RET_CHUNK = 64
GLA_CHUNK = 16
ROPE_BASE = 10000.0
RWKV_DECAY_RANK = 64
RWKV_A_RANK = 64
RWKV_V_RANK = 32
RWKV_G_RANK = 160
S5_CH = 16
S5_GROUPS = GROUP_W // S5_CH
S5_STATE = 64
NORM_EPS = 1e-6
RWKV_LN_EPS = 64e-5
RET_COLS = 4 * GROUP_W
HGRN_COLS = 4 * GROUP_W
RWKV_SPLITS = (GROUP_W, GROUP_W, GROUP_W, RWKV_DECAY_RANK, RWKV_A_RANK, RWKV_G_RANK)
RWKV_COLS = sum(RWKV_SPLITS)
S5_COLS = GROUP_W
N_IN = RET_COLS + HGRN_COLS + RWKV_COLS + S5_COLS

kernel_name = 'hybrid_parallel_heads_ret_hgrn2_rwkv7_s5'


def split_cols(z, widths):
    idx = [int(i) for i in np.cumsum(widths)[:-1]]
    return jnp.split(z, idx, axis=-1)


def heads(t):
    return t.reshape(t.shape[:-1] + (t.shape[-1] // HEAD_DIM, HEAD_DIM))


def rmsnorm(x, g):
    xf = x.astype(jnp.float32)
    y = xf * lax.rsqrt(jnp.mean(xf * xf, axis=-1, keepdims=True) + NORM_EPS)
    return (y * g.astype(jnp.float32)).astype(x.dtype)


def rms_heads(o):
    return o * lax.rsqrt(jnp.mean(o * o, axis=-1, keepdims=True) + NORM_EPS)


def rotary(t, pos):
    half = HEAD_DIM // 2
    inv_freq = ROPE_BASE ** (-jnp.arange(half, dtype=jnp.float32) / half)
    ang = pos.astype(jnp.float32)[:, None] * inv_freq[None, :]
    cos, sin = jnp.cos(ang)[None, :, None, :], jnp.sin(ang)[None, :, None, :]
    t1, t2 = t[..., :half], t[..., half:]
    return jnp.concatenate([t1 * cos - t2 * sin, t1 * sin + t2 * cos], axis=-1)


def over_segments(chunk_fn, arrays, state0, chunk):
    out_meta, state = chunk_fn(*[a[:, :N_META] for a in arrays], state0, N_META)
    out_real, _ = chunk_fn(*[a[:, N_META:] for a in arrays], state, chunk)
    return jnp.concatenate([out_meta, out_real], axis=1)


def retention_chunks(q, k, v, state, chunk):
    B_, T, H, d = q.shape
    n = T // chunk
    log_g = jnp.log1p(-jnp.exp2(-5.0 - jnp.arange(H, dtype=jnp.float32)))
    qc, kc, vc = (a.reshape(B_, n, chunk, H, d) for a in (q, k, v))
    t = jnp.arange(chunk, dtype=jnp.float32)
    rel = t[:, None] - t[None, :]
    causal = rel >= 0
    dmask = jnp.where(causal[None], jnp.exp(jnp.where(causal, rel, 0.0)[None] * log_g[:, None, None]), 0.0)
    scores = jnp.einsum('bnthd,bnshd->bnhts', qc, kc) * dmask
    intra = jnp.einsum('bnhts,bnshd->bnthd', scores, vc)
    k_dec = jnp.exp((chunk - 1.0 - t)[:, None] * log_g[None, :])
    q_dec = jnp.exp((t + 1.0)[:, None] * log_g[None, :])
    kv = jnp.einsum('bnshd,bnshe->nbhde', kc * k_dec[:, :, None], vc)
    q_in = jnp.moveaxis(qc * q_dec[:, :, None], 1, 0)
    g_chunk = jnp.exp(chunk * log_g)[None, :, None, None]

    def step(S, inp):
        q_n, kv_n = inp
        return g_chunk * S + kv_n, jnp.einsum('bthd,bhde->bthe', q_n, S)

    state, inter = lax.scan(step, state, (q_in, kv))
    out = intra + jnp.moveaxis(inter, 0, 1)
    return out.reshape(B_, T, H, d), state


def gla_chunks(q, k, v, log_f, state, chunk):
    B_, T, H, dk = q.shape
    n = T // chunk
    rs = lambda a: a.reshape(B_, n, chunk, H, a.shape[-1])
    qc, kc, vc = rs(q), rs(k), rs(v)
    b = jnp.cumsum(rs(log_f), axis=2)
    causal = jnp.tril(jnp.ones((chunk, chunk), dtype=bool))[None, None, :, :, None, None]
    diff = b[:, :, :, None] - b[:, :, None, :]
    dec = jnp.exp(jnp.where(causal, diff, -jnp.inf))
    scores = jnp.sum(qc[:, :, :, None] * kc[:, :, None] * dec, axis=-1)
    intra = jnp.einsum('bntsh,bnshe->bnthe', scores, vc)
    b_last = b[:, :, -1:]
    kv = jnp.einsum('bnshd,bnshe->nbhde', kc * jnp.exp(b_last - b), vc)
    q_in = jnp.moveaxis(qc * jnp.exp(b), 1, 0)
    f_chunk = jnp.moveaxis(jnp.exp(b_last[:, :, 0]), 1, 0)[..., None]

    def step(S, inp):
        q_n, kv_n, f_n = inp
        return f_n * S + kv_n, jnp.einsum('bthd,bhde->bthe', q_n, S)

    state, inter = lax.scan(step, state, (q_in, kv, f_chunk))
    out = intra + jnp.moveaxis(inter, 0, 1)
    return out.reshape(B_, T, H, v.shape[-1]), state


def rwkv7_scan(r, w, k, v, kk, a):
    B_, L, H, d = r.shape
    S0 = jnp.zeros((B_, H, d, d), jnp.float32)

    def step(S, inp):
        r_t, w_t, k_t, v_t, kk_t, a_t = inp
        sa = jnp.einsum('bhvk,bhk->bhv', S, -kk_t)
        S = S * w_t[:, :, None, :] + sa[..., None] * (kk_t * a_t)[:, :, None, :] + v_t[..., None] * k_t[:, :, None, :]
        return S, jnp.einsum('bhvk,bhk->bhv', S, r_t)

    _, o = lax.scan(step, S0, tuple(jnp.moveaxis(t, 1, 0) for t in (r, w, k, v, kk, a)))
    return jnp.moveaxis(o, 0, 1)


def complex_affine_combine(e1, e2):
    a1r, a1i, b1r, b1i = e1
    a2r, a2i, b2r, b2i = e2
    return (a2r * a1r - a2i * a1i,
            a2r * a1i + a2i * a1r,
            a2r * b1r - a2i * b1i + b2r,
            a2r * b1i + a2i * b1r + b2i)


def retention_mixer(z, pos):
    f32 = jnp.float32
    q, k, v, g = split_cols(z.astype(f32), (GROUP_W,) * 4)
    q = rotary(heads(q), pos)
    k = rotary(heads(k), pos) * HEAD_DIM ** -0.5
    s0 = jnp.zeros((z.shape[0], N_HEADS, HEAD_DIM, HEAD_DIM), f32)
    o = over_segments(retention_chunks, (q, k, heads(v)), s0, RET_CHUNK)
    o = rms_heads(o).reshape(z.shape[:-1] + (GROUP_W,))
    return (o * jax.nn.silu(g)).astype(z.dtype)


def hgrn2_mixer(z, lower_bound, norm_g):
    f32 = jnp.float32
    q, f, i, g = split_cols(z.astype(f32), (GROUP_W,) * 4)
    forget = lower_bound.astype(f32) + (1.0 - lower_bound.astype(f32)) * jax.nn.sigmoid(f)
    k = 1.0 - forget
    q = jax.nn.silu(q) * HEAD_DIM ** -0.5
    s0 = jnp.zeros((z.shape[0], N_HEADS, HEAD_DIM, HEAD_DIM), f32)
    o = over_segments(gla_chunks, (heads(q), heads(k), heads(i), heads(jnp.log(forget))), s0, GLA_CHUNK)
    o = (rms_heads(o) * norm_g.astype(f32)).reshape(z.shape[:-1] + (GROUP_W,))
    return (o * jax.nn.silu(g)).astype(z.dtype)


def rwkv7_mixer(z, v_first, vmix, mu, w0, w_up, a0, a_up, g_up, k_k, k_a, r_k, ln_w, ln_b):
    f32 = jnp.float32
    zf = z.astype(f32)
    prev = jnp.pad(zf, ((0, 0), (1, 0), (0, 0)))[:, :-1]
    zf = zf + (prev - zf) * mu.astype(f32)
    r, k, v, wl, al, gl = split_cols(zf, RWKV_SPLITS)
    w_log = -jax.nn.softplus(-(w0 + jnp.tanh(wl) @ w_up)) - 0.5
    decay = jnp.exp(-jnp.exp(w_log.astype(f32)))
    a = jax.nn.sigmoid(a0 + al @ a_up).astype(f32)
    g = (jax.nn.sigmoid(gl) @ g_up).astype(f32)
    if vmix is None:
        v_first = v
    else:
        v0, v_down, v_up = vmix
        v = v + (v_first - v) * jax.nn.sigmoid(v0 + (v @ v_down) @ v_up).astype(f32)
    kk = heads(k * k_k.astype(f32))
    kk = kk / jnp.maximum(jnp.sqrt(jnp.sum(kk * kk, axis=-1, keepdims=True)), 1e-12)
    k = k * (1.0 + (a - 1.0) * k_a.astype(f32))
    rh, kh, vh = heads(r), heads(k), heads(v)
    o = rwkv7_scan(rh, heads(decay), kh, vh, kk, heads(a))
    mean = jnp.mean(o, axis=-1, keepdims=True)
    var = jnp.mean(jnp.square(o - mean), axis=-1, keepdims=True)
    o = ((o - mean) * lax.rsqrt(var + RWKV_LN_EPS)).reshape(z.shape[:-1] + (GROUP_W,))
    o = heads(o * ln_w.astype(f32) + ln_b.astype(f32))
    o = o + jnp.sum(rh * kh * r_k.astype(f32), axis=-1, keepdims=True) * vh
    o = o.reshape(z.shape[:-1] + (GROUP_W,)) * g
    return o.astype(z.dtype), v_first


def s5_mixer(z, a_re, a_im, log_dt, b_re, b_im, c_re, c_im, d_skip, glu_w, glu_b):
    f32 = jnp.float32
    B_, L, _ = z.shape
    u = z.astype(f32)
    ug = u.reshape(B_, L, S5_GROUPS, S5_CH)
    lam_re, lam_im = a_re.astype(f32), a_im.astype(f32)
    dt = jnp.exp(log_dt.astype(f32))[:, None]
    mag, ph = jnp.exp(lam_re * dt), lam_im * dt
    ab_re, ab_im = mag * jnp.cos(ph), mag * jnp.sin(ph)
    den = lam_re * lam_re + lam_im * lam_im
    nr, ni = ab_re - 1.0, ab_im
    zc_re = (nr * lam_re + ni * lam_im) / den
    zc_im = (ni * lam_re - nr * lam_im) / den
    b_re, b_im = b_re.astype(f32), b_im.astype(f32)
    bb_re = zc_re[..., None] * b_re - zc_im[..., None] * b_im
    bb_im = zc_re[..., None] * b_im + zc_im[..., None] * b_re
    bu_re = jnp.einsum('blgc,gpc->blgp', ug, bb_re)
    bu_im = jnp.einsum('blgc,gpc->blgp', ug, bb_im)
    elems = (jnp.broadcast_to(ab_re, bu_re.shape), jnp.broadcast_to(ab_im, bu_re.shape), bu_re, bu_im)
    _, _, x_re, x_im = lax.associative_scan(complex_affine_combine, elems, axis=1)
    y = jnp.einsum('blgp,gcp->blgc', x_re, c_re.astype(f32)) - jnp.einsum('blgp,gcp->blgc', x_im, c_im.astype(f32))
    y = jax.nn.gelu(y.reshape(B_, L, GROUP_W) + d_skip.astype(f32) * u)
    return (y * jax.nn.sigmoid(y @ glu_w.astype(f32) + glu_b.astype(f32))).astype(z.dtype)


def setup_inputs(seed: int = 0) -> dict:
    key = jax.random.key(seed)
    ks = iter(jax.random.split(key, 48))
    f32 = jnp.float32
    nrm = lambda shape, scale: jax.random.normal(next(ks), shape, f32) * scale
    uni = lambda shape, lo, hi: jax.random.uniform(next(ks), shape, f32, lo, hi)
    G, P = S5_GROUPS, S5_STATE
    return {
        'x': nrm((BATCH, SEQ, D_MODEL), 1.0),
        'meta_tokens': nrm((N_META, D_MODEL), 1.0),
        'norm_mix_g': 1.0 + nrm((DEPTH, D_MODEL), 0.02),
        'w_in': nrm((DEPTH, D_MODEL, N_IN), D_MODEL ** -0.5),
        'hgrn_lb_logits': nrm((DEPTH, GROUP_W), 0.1),
        'hgrn_norm_g': 1.0 + nrm((DEPTH, HEAD_DIM), 0.02),
        'rwkv_mu': uni((DEPTH, RWKV_COLS), 0.0, 1.0),
        'rwkv_w0': uni((DEPTH, GROUP_W), -6.0, -1.0),
        'rwkv_w_up': nrm((DEPTH, RWKV_DECAY_RANK, GROUP_W), RWKV_DECAY_RANK ** -0.5),
        'rwkv_a0': nrm((DEPTH, GROUP_W), 0.1),
        'rwkv_a_up': nrm((DEPTH, RWKV_A_RANK, GROUP_W), RWKV_A_RANK ** -0.5),
        'rwkv_g_up': nrm((DEPTH, RWKV_G_RANK, GROUP_W), RWKV_G_RANK ** -0.5),
        'rwkv_k_k': 0.85 + nrm((DEPTH, GROUP_W), 0.02),
        'rwkv_k_a': 1.0 + nrm((DEPTH, GROUP_W), 0.02),
        'rwkv_r_k': nrm((DEPTH, N_HEADS, HEAD_DIM), 0.1),
        'rwkv_ln_w': 1.0 + nrm((DEPTH, GROUP_W), 0.02),
        'rwkv_ln_b': nrm((DEPTH, GROUP_W), 0.02),
        'rwkv_v0': nrm((DEPTH - 1, GROUP_W), 0.1),
        'rwkv_v_down': nrm((DEPTH - 1, GROUP_W, RWKV_V_RANK), GROUP_W ** -0.5),
        'rwkv_v_up': nrm((DEPTH - 1, RWKV_V_RANK, GROUP_W), RWKV_V_RANK ** -0.5),
        's5_a_re': -0.5 + nrm((DEPTH, G, P), 0.01),
        's5_a_im': jnp.pi * jnp.arange(P, dtype=f32)[None, None, :] + nrm((DEPTH, G, P), 0.01),
        's5_log_dt': uni((DEPTH, G), math.log(1e-3), math.log(1e-1)),
        's5_b_re': nrm((DEPTH, G, P, S5_CH), (2 * S5_CH) ** -0.5),
        's5_b_im': nrm((DEPTH, G, P, S5_CH), (2 * S5_CH) ** -0.5),
        's5_c_re': nrm((DEPTH, G, S5_CH, P), P ** -0.5),
        's5_c_im': nrm((DEPTH, G, S5_CH, P), P ** -0.5),
        's5_d': nrm((DEPTH, GROUP_W), 1.0),
        's5_glu_w': nrm((DEPTH, GROUP_W, GROUP_W), GROUP_W ** -0.5),
        's5_glu_b': nrm((DEPTH, GROUP_W), 0.02),
        'w_out': nrm((DEPTH, D_MIX, D_MODEL), D_MIX ** -0.5),
        'norm_ffn_g': 1.0 + nrm((DEPTH, D_MODEL), 0.02),
        'w_ffn_up': nrm((DEPTH, D_MODEL, D_FF), D_MODEL ** -0.5),
        'w_ffn_down': nrm((DEPTH, D_FF, D_MODEL), D_FF ** -0.5),
        'norm_f_g': 1.0 + nrm((D_MODEL,), 0.02),
    }


def reference(x, meta_tokens, norm_mix_g, w_in, hgrn_lb_logits, hgrn_norm_g, rwkv_mu, rwkv_w0, rwkv_w_up,
              rwkv_a0, rwkv_a_up, rwkv_g_up, rwkv_k_k, rwkv_k_a, rwkv_r_k, rwkv_ln_w, rwkv_ln_b, rwkv_v0,
              rwkv_v_down, rwkv_v_up, s5_a_re, s5_a_im, s5_log_dt, s5_b_re, s5_b_im, s5_c_re, s5_c_im, s5_d,
              s5_glu_w, s5_glu_b, w_out, norm_ffn_g, w_ffn_up, w_ffn_down, norm_f_g):
    B_ = x.shape[0]
    meta = jnp.broadcast_to(meta_tokens.astype(x.dtype)[None], (B_, N_META, D_MODEL))
    h = jnp.concatenate([meta, x], axis=1)
    pos = jnp.arange(h.shape[1])
    p = jax.nn.softmax(hgrn_lb_logits.astype(jnp.float32), axis=0)
    lower_bounds = jnp.cumsum(p, axis=0) - p[0]
    v_first = None
    for l in range(DEPTH):
        xn = rmsnorm(h, norm_mix_g[l])
        z_ret, z_hgrn, z_rwkv, z_s5 = split_cols(xn @ w_in[l], (RET_COLS, HGRN_COLS, RWKV_COLS, S5_COLS))
        o_ret = retention_mixer(z_ret, pos)
        o_hgrn = hgrn2_mixer(z_hgrn, lower_bounds[l], hgrn_norm_g[l])
        vmix = None if l == 0 else (rwkv_v0[l - 1], rwkv_v_down[l - 1], rwkv_v_up[l - 1])
        o_rwkv, v_first = rwkv7_mixer(z_rwkv, v_first, vmix, rwkv_mu[l], rwkv_w0[l], rwkv_w_up[l], rwkv_a0[l],
                                      rwkv_a_up[l], rwkv_g_up[l], rwkv_k_k[l], rwkv_k_a[l], rwkv_r_k[l],
                                      rwkv_ln_w[l], rwkv_ln_b[l])
        o_s5 = s5_mixer(z_s5, s5_a_re[l], s5_a_im[l], s5_log_dt[l], s5_b_re[l], s5_b_im[l], s5_c_re[l],
                        s5_c_im[l], s5_d[l], s5_glu_w[l], s5_glu_b[l])
        h = h + jnp.concatenate([o_ret, o_hgrn, o_rwkv, o_s5], axis=-1) @ w_out[l]
        hn = rmsnorm(h, norm_ffn_g[l])
        h = h + jnp.square(jax.nn.relu(hn @ w_ffn_up[l])) @ w_ffn_down[l]
    return rmsnorm(h, norm_f_g)[:, N_META:]
```

```python
import functools

import jax
import jax.numpy as jnp
from jax import lax
from jax.experimental import pallas as pl
from jax.experimental.pallas import tpu as pltpu

F32 = jnp.float32
BF16 = jnp.bfloat16

D_MODEL = 1024
N_META = 16
HEAD_DIM = 64
GROUP_W = 256
N_HEADS = 4
D_FF = 4096
ROPE_BASE = 10000.0
NORM_EPS = 1e-6
RWKV_LN_EPS = 64e-5
S5_GROUPS = 16
S5_CH = 16
S5_STATE = 64
S5_LANES = S5_GROUPS * S5_STATE
RWKV_COLS = 1056
RWKV_COLS_PAD = 1152
N_Z = 1024 + 1024 + RWKV_COLS_PAD + 256

CHUNK = 128
PAD_FRONT = CHUNK - N_META
VMEM_LIMIT = 56 * 1024 * 1024


def _dot(a, b):
    return jnp.dot(a.astype(BF16), b.astype(BF16), preferred_element_type=F32)


def _dot_f32(a, b):
    return jnp.dot(a, b, preferred_element_type=F32, precision=lax.Precision.HIGHEST)


def _dot_nt(a, b):
    return lax.dot_general(a.astype(BF16), b.astype(BF16), (((1,), (1,)), ((), ())),
                           preferred_element_type=F32)


def _split_dot(x, w_bf16):
    hi = x.astype(BF16)
    lo = (x - hi.astype(F32)).astype(BF16)
    return (jnp.dot(hi, w_bf16, preferred_element_type=F32)
            + jnp.dot(lo, w_bf16, preferred_element_type=F32))


def _iota(shape, dim):
    return lax.broadcasted_iota(jnp.int32, shape, dim)


def _head_masks():
    lane = _iota((1, GROUP_W), 1)
    return [((lane >> 6) == h).astype(F32) for h in range(N_HEADS)]


def _block_diag_mask():
    return (_iota((GROUP_W, GROUP_W), 0) >> 6) == (_iota((GROUP_W, GROUP_W), 1) >> 6)


def _head_sum(x, bd_ones):
    return _split_dot(x, bd_ones)


def _silu(x):
    return x * jax.nn.sigmoid(x)


def _in_proj_kernel(x_ref, g_ref, w_ref, zr_ref, zh_ref, zw_ref, zs_ref, *, tm):
    x = x_ref[...]
    ms = jnp.mean(x * x, axis=-1, keepdims=True)
    xn = x * lax.rsqrt(ms + NORM_EPS) * g_ref[...]
    row = pl.program_id(1) * tm + _iota((tm, 1), 0)
    xb = jnp.where(row < PAD_FRONT, 0.0, xn).astype(BF16)
    zr_ref[...] = jnp.dot(xb, w_ref[:, 0:1024], preferred_element_type=F32)
    zh_ref[...] = jnp.dot(xb, w_ref[:, 1024:2048], preferred_element_type=F32)
    zw_ref[...] = jnp.dot(xb, w_ref[:, 2048:2048 + RWKV_COLS_PAD], preferred_element_type=F32)
    zs_ref[...] = jnp.dot(xb, w_ref[:, 2048 + RWKV_COLS_PAD:N_Z], preferred_element_type=F32)


def _in_proj(h, g, w, tm):
    B, Lp, _ = h.shape
    row_spec = lambda n: pl.BlockSpec((None, tm, n), lambda b, j: (b, j, 0))
    full = lambda shape: pl.BlockSpec(shape, lambda b, j: (0,) * len(shape))
    return pl.pallas_call(
        functools.partial(_in_proj_kernel, tm=tm),
        grid=(B, Lp // tm),
        in_specs=[row_spec(D_MODEL), full((1, D_MODEL)), full((D_MODEL, N_Z))],
        out_specs=[row_spec(1024), row_spec(1024), row_spec(RWKV_COLS_PAD), row_spec(256)],
        out_shape=[jax.ShapeDtypeStruct((B, Lp, n), F32) for n in (1024, 1024, RWKV_COLS_PAD, 256)],
        compiler_params=pltpu.CompilerParams(dimension_semantics=("parallel", "parallel"),
                                             vmem_limit_bytes=VMEM_LIMIT),
        name="in_proj",
    )(h, g, w)


def _ffn_kernel(h_ref, o1_ref, o2_ref, o3_ref, o4_ref, wo_ref, g_ref, wu_ref, wd_ref, gf_ref, out_ref,
                *, final, fc):
    o = jnp.concatenate([o1_ref[...], o2_ref[...], o3_ref[...], o4_ref[...]], axis=1).astype(BF16)
    acc = h_ref[...] + jnp.dot(o, wo_ref[...], preferred_element_type=F32)
    ms = jnp.mean(acc * acc, axis=-1, keepdims=True)
    hb = (acc * lax.rsqrt(ms + NORM_EPS) * g_ref[...]).astype(BF16)
    mlp = None
    for c in range(0, D_FF, fc):
        u = jnp.dot(hb, wu_ref[:, c:c + fc], preferred_element_type=F32)
        u = jnp.square(jnp.maximum(u, 0.0)).astype(BF16)
        d = jnp.dot(u, wd_ref[c:c + fc, :], preferred_element_type=F32)
        mlp = d if mlp is None else mlp + d
    y = acc + mlp
    if final:
        ms = jnp.mean(y * y, axis=-1, keepdims=True)
        y = y * lax.rsqrt(ms + NORM_EPS) * gf_ref[...]
    out_ref[...] = y


def _out_ffn(h, outs, wo, g, wu, wd, gf, tm, final):
    B, Lp, _ = h.shape
    row_spec = lambda n: pl.BlockSpec((None, tm, n), lambda b, j: (b, j, 0))
    const = lambda shape: pl.BlockSpec(shape, lambda b, j: (0,) * len(shape))
    return pl.pallas_call(
        functools.partial(_ffn_kernel, final=final, fc=1024),
        grid=(B, Lp // tm),
        in_specs=[row_spec(D_MODEL)] + [row_spec(GROUP_W)] * 4 + [
            const((D_MODEL, D_MODEL)), const((1, D_MODEL)), const((D_MODEL, D_FF)),
            const((D_FF, D_MODEL)), const((1, D_MODEL))],
        out_specs=row_spec(D_MODEL),
        out_shape=jax.ShapeDtypeStruct((B, Lp, D_MODEL), F32),
        compiler_params=pltpu.CompilerParams(dimension_semantics=("parallel", "parallel"),
                                             vmem_limit_bytes=VMEM_LIMIT),
        name="out_ffn",
    )(h, *outs, wo, g, wu, wd, gf)


def _ret_kernel(z_ref, cos_ref, sin_ref, dm_ref, qdec_ref, kdec_t_ref, gcm_ref, o_ref, s_scr):
    @pl.when(pl.program_id(1) == 0)
    def _():
        s_scr[...] = jnp.zeros_like(s_scr)

    hm = _head_masks()
    bdm = _block_diag_mask()
    bd_ones = bdm.astype(BF16)
    z = z_ref[...]
    q, k, v, g = z[:, 0:256], z[:, 256:512], z[:, 512:768], z[:, 768:1024]
    cos, sin = cos_ref[...], sin_ref[...]
    upper = (_iota((1, GROUP_W), 1) & (HEAD_DIM - 1)) >= HEAD_DIM // 2

    def rot(t):
        swapped = jnp.where(upper, pltpu.roll(t, HEAD_DIM // 2, 1),
                            pltpu.roll(t, GROUP_W - HEAD_DIM // 2, 1))
        return t * cos + swapped * sin

    qr = rot(q)
    kr = rot(k) * HEAD_DIM ** -0.5
    s = s_scr[...]
    o = _dot(qr * qdec_ref[...], s)
    krb = kr.astype(BF16)
    vb = v.astype(BF16)
    for h in range(N_HEADS):
        sc = _dot_nt(qr * hm[h], krb) * dm_ref[h]
        o = o + jnp.dot(sc.astype(BF16), vb, preferred_element_type=F32) * hm[h]
    kv = _dot(kr.T * kdec_t_ref[...], vb)
    s_scr[...] = gcm_ref[...] * s + jnp.where(bdm, kv, 0.0)
    msq = _head_sum(o * o, bd_ones) * (1.0 / HEAD_DIM)
    o_ref[...] = o * lax.rsqrt(msq + NORM_EPS) * _silu(g)


def _retention(z, cos, sin, dm, qdec, kdec_t, gcm):
    B, Lp, _ = z.shape
    C = CHUNK
    const = lambda shape: pl.BlockSpec(shape, lambda b, j: (0,) * len(shape))
    return pl.pallas_call(
        _ret_kernel,
        grid=(B, Lp // C),
        in_specs=[pl.BlockSpec((None, C, 1024), lambda b, j: (b, j, 0)),
                  pl.BlockSpec((C, GROUP_W), lambda b, j: (j, 0)),
                  pl.BlockSpec((C, GROUP_W), lambda b, j: (j, 0)),
                  const((N_HEADS, C, C)), const((C, GROUP_W)), const((GROUP_W, C)),
                  const((GROUP_W, GROUP_W))],
        out_specs=pl.BlockSpec((None, C, GROUP_W), lambda b, j: (b, j, 0)),
        out_shape=jax.ShapeDtypeStruct((B, Lp, GROUP_W), F32),
        scratch_shapes=[pltpu.VMEM((GROUP_W, GROUP_W), F32)],
        compiler_params=pltpu.CompilerParams(dimension_semantics=("parallel", "arbitrary")),
        name="retention",
    )(z, cos, sin, dm, qdec, kdec_t, gcm)


def _gla_kernel(z_ref, lb_ref, ng_ref, o_ref, s_scr):
    C = CHUNK

    @pl.when(pl.program_id(1) == 0)
    def _():
        s_scr[...] = jnp.zeros_like(s_scr)

    hm = _head_masks()
    bdm = _block_diag_mask()
    bd_ones = bdm.astype(BF16)
    z = z_ref[...]
    q, f, v, g = z[:, 0:256], z[:, 256:512], z[:, 512:768], z[:, 768:1024]
    lb = lb_ref[...]
    forget = lb + (1.0 - lb) * jax.nn.sigmoid(f)
    k = 1.0 - forget
    logf = jnp.log(forget)
    q = _silu(q) * HEAD_DIM ** -0.5
    row = _iota((C, 1), 0)
    ti, si = _iota((C, C), 0), _iota((C, C), 1)
    b = _dot_f32((ti >= si).astype(F32), logf)

    terms, shifted_v = [q * k], [v]
    for d in range(1, 8):
        valid = (row & 7) >= d
        diff = jnp.where(valid, b - pltpu.roll(b, d, 0), 0.0)
        terms.append(jnp.where(valid, q * pltpu.roll(k, d, 0) * jnp.exp(diff), 0.0))
        shifted_v.append(pltpu.roll(v, d, 0))
    band = _head_sum(jnp.concatenate(terms, axis=0), bd_ones)
    o = band[0:C] * shifted_v[0]
    for d in range(1, 8):
        o = o + band[d * C:(d + 1) * C] * shifted_v[d]

    p = jnp.zeros((N_HEADS * C, C), F32)
    m = 8
    while 2 * m <= C:
        b3 = b.reshape(C // (2 * m), 2 * m, GROUP_W)
        ref = b3[:, m - 1:m, :]
        dq = jnp.minimum(b3 - ref, 0.0).reshape(C, GROUP_W)
        dk = jnp.minimum(ref - b3, 0.0).reshape(C, GROUP_W)
        second = (row & (2 * m - 1)) >= m
        qm = jnp.where(second, q * jnp.exp(dq), 0.0)
        km = jnp.where(second, 0.0, k * jnp.exp(dk))
        lq = jnp.concatenate([qm * hm[h] for h in range(N_HEADS)], axis=0)
        sc = _dot(lq, km.T)
        same = (ti // (2 * m)) == (si // (2 * m))
        p = p + jnp.where(jnp.concatenate([same] * N_HEADS, axis=0), sc, 0.0)
        m *= 2
    vb = v.astype(BF16)
    for h in range(N_HEADS):
        o = o + jnp.dot(p[h * C:(h + 1) * C].astype(BF16), vb, preferred_element_type=F32) * hm[h]

    s = s_scr[...]
    o = o + _dot(q * jnp.exp(b), s)
    b_t = b.T
    b_last = b_t[:, C - 1:C]
    k_end = k.T * jnp.exp(b_last - b_t)
    s_scr[...] = jnp.exp(b_last) * s + jnp.where(bdm, _dot(k_end, vb), 0.0)

    msq = _head_sum(o * o, bd_ones) * (1.0 / HEAD_DIM)
    o_ref[...] = o * lax.rsqrt(msq + NORM_EPS) * ng_ref[...] * _silu(g)


def _hgrn2(z, lb, ng):
    B, Lp, _ = z.shape
    C = CHUNK
    const = lambda shape: pl.BlockSpec(shape, lambda b, j: (0,) * len(shape))
    return pl.pallas_call(
        _gla_kernel,
        grid=(B, Lp // C),
        in_specs=[pl.BlockSpec((None, C, 1024), lambda b, j: (b, j, 0)),
                  const((1, GROUP_W)), const((1, GROUP_W))],
        out_specs=pl.BlockSpec((None, C, GROUP_W), lambda b, j: (b, j, 0)),
        out_shape=jax.ShapeDtypeStruct((B, Lp, GROUP_W), F32),
        scratch_shapes=[pltpu.VMEM((GROUP_W, GROUP_W), F32)],
        compiler_params=pltpu.CompilerParams(dimension_semantics=("parallel", "arbitrary")),
        name="hgrn2",
    )(z, lb, ng)


def _rwkv_kernel(*refs, has_vmix):
    C = CHUNK
    if has_vmix:
        (z_ref, vf_ref, mu_ref, w0_ref, wup_ref, a0_ref, aup_ref, gup_ref, kk_ref, ka_ref, rk_ref,
         lnw_ref, lnb_ref, v0_ref, vdn_ref, vup_ref, o_ref, t_scr, carry_scr) = refs
    else:
        (z_ref, mu_ref, w0_ref, wup_ref, a0_ref, aup_ref, gup_ref, kk_ref, ka_ref, rk_ref,
         lnw_ref, lnb_ref, o_ref, vf_out_ref, t_scr, carry_scr) = refs

    @pl.when(pl.program_id(1) == 0)
    def _():
        t_scr[...] = jnp.zeros_like(t_scr)
        carry_scr[...] = jnp.zeros_like(carry_scr)

    hm = _head_masks()
    bdm = _block_diag_mask()
    bd_ones = bdm.astype(BF16)
    row = _iota((C, 1), 0)
    ti, si = _iota((C, C), 0), _iota((C, C), 1)

    z = z_ref[...]
    z_prev = jnp.where(row == 0, carry_scr[0:1, :], pltpu.roll(z, 1, 0))
    carry_scr[0:1, :] = z[C - 1:C, :]
    zf = z + (z_prev - z) * mu_ref[...]
    r, k, v = zf[:, 0:256], zf[:, 256:512], zf[:, 512:768]
    wa, gl = zf[:, 768:896], zf[:, 896:RWKV_COLS_PAD]

    w_log = -jax.nn.softplus(-(w0_ref[...] + _dot(jnp.tanh(wa), wup_ref[...]))) - 0.5
    logw = -jnp.exp(w_log)
    a = jax.nn.sigmoid(a0_ref[...] + _dot(wa, aup_ref[...]))
    gate = _dot(jax.nn.sigmoid(gl), gup_ref[...])
    if has_vmix:
        mix = jax.nn.sigmoid(v0_ref[...] + _dot(_dot(v, vdn_ref[...]), vup_ref[...]))
        v = v + (vf_ref[...] - v) * mix
    else:
        vf_out_ref[...] = v
    kk = k * kk_ref[...]
    kk = kk / jnp.maximum(jnp.sqrt(_head_sum(kk * kk, bd_ones)), 1e-12)
    k = k * (1.0 + (a - 1.0) * ka_ref[...])
    alpha = -kk * a

    gc = _dot_f32((ti >= si).astype(F32), logw)
    gp = gc - logw
    gm = gc[C // 2:C // 2 + 1, :]
    r_mid, b_mid = r * jnp.exp(gc - gm), kk * jnp.exp(gp - gm)
    r_abs, b_abs = r * jnp.exp(gc), kk * jnp.exp(gp)
    g_t, a_t, k_t = gc.T, alpha.T, k.T
    from_mid = jnp.exp(g_t[:, C // 2:C // 2 + 1] - g_t)
    g_last = g_t[:, C - 1:C]
    to_end = jnp.exp(g_last - g_t)

    lhs = jnp.concatenate([b_mid * hm[h] for h in range(N_HEADS)]
                          + [r_mid * hm[h] for h in range(N_HEADS)], axis=0)
    rhs_t = jnp.concatenate([a_t * from_mid, k_t * from_mid], axis=1)
    sc = _dot(lhs, rhs_t)

    t0 = t_scr[...]
    t0b = t0.astype(BF16)
    vb = v.astype(BF16)
    strict, incl = ti > si, ti >= si
    rhs = jnp.dot(b_abs.astype(BF16), t0b, preferred_element_type=F32)
    inv_minus_eye = []
    for h in range(N_HEADS):
        blk = sc[h * C:(h + 1) * C]
        a_mat = jnp.where(strict, blk[:, 0:C], 0.0)
        rhs = rhs + jnp.dot(jnp.where(strict, blk[:, C:2 * C], 0.0).astype(BF16), vb,
                            preferred_element_type=F32) * hm[h]
        pw, n = a_mat, a_mat
        for _ in range(C.bit_length() - 2):
            pwb = pw.astype(BF16)
            pw = jnp.dot(pwb, pwb, preferred_element_type=F32)
            n = n + pw + _dot(n, pw)
        inv_minus_eye.append(n)
    u = rhs
    rhsb = rhs.astype(BF16)
    for h in range(N_HEADS):
        u = u + jnp.dot(inv_minus_eye[h].astype(BF16), rhsb, preferred_element_type=F32) * hm[h]
    uv = jnp.concatenate([u, v], axis=0).astype(BF16)
    o = jnp.dot(r_abs.astype(BF16), t0b, preferred_element_type=F32)
    for h in range(N_HEADS):
        blk = jnp.where(jnp.concatenate([incl, incl], axis=1), sc[(N_HEADS + h) * C:(N_HEADS + h + 1) * C], 0.0)
        o = o + jnp.dot(blk.astype(BF16), uv, preferred_element_type=F32) * hm[h]
    upd = _dot(jnp.concatenate([a_t * to_end, k_t * to_end], axis=1), uv)
    t_scr[...] = jnp.exp(g_last) * t0 + jnp.where(bdm, upd, 0.0)

    inv_d = 1.0 / HEAD_DIM
    dev = o - _head_sum(o, bd_ones) * inv_d
    var = _head_sum(dev * dev, bd_ones) * inv_d
    on = dev * lax.rsqrt(var + RWKV_LN_EPS) * lnw_ref[...] + lnb_ref[...]
    on = on + _head_sum(r * k * rk_ref[...], bd_ones) * v
    o_ref[...] = on * gate


def _rwkv7(z, v_first, params, vmix):
    B, Lp, _ = z.shape
    C = CHUNK
    has_vmix = vmix is not None
    const = lambda a: pl.BlockSpec(a.shape, lambda b, j: (0,) * a.ndim)
    tok = lambda n: pl.BlockSpec((None, C, n), lambda b, j: (b, j, 0))
    args, specs = [z], [tok(RWKV_COLS_PAD)]
    if has_vmix:
        args.append(v_first)
        specs.append(tok(GROUP_W))
    extra = list(params) + (list(vmix) if has_vmix else [])
    args += extra
    specs += [const(a) for a in extra]
    o_shape = jax.ShapeDtypeStruct((B, Lp, GROUP_W), F32)
    res = pl.pallas_call(
        functools.partial(_rwkv_kernel, has_vmix=has_vmix),
        grid=(B, Lp // C),
        in_specs=specs,
        out_specs=tok(GROUP_W) if has_vmix else [tok(GROUP_W), tok(GROUP_W)],
        out_shape=o_shape if has_vmix else [o_shape, o_shape],
        scratch_shapes=[pltpu.VMEM((GROUP_W, GROUP_W), F32), pltpu.VMEM((8, RWKV_COLS_PAD), F32)],
        compiler_params=pltpu.CompilerParams(dimension_semantics=("parallel", "arbitrary")),
        name="rwkv7",
    )(*args)
    return (res, v_first) if has_vmix else (res[0], res[1])


def _s5_kernel(u_ref, bb_ref, cc_ref, pw_re_ref, pw_im_ref, tab_re_ref, tab_im_ref, d_ref, gw_ref, gb_ref,
               o_ref, c_re_scr, c_im_scr):
    C = CHUNK

    @pl.when(pl.program_id(1) == 0)
    def _():
        c_re_scr[...] = jnp.zeros_like(c_re_scr)
        c_im_scr[...] = jnp.zeros_like(c_im_scr)

    u = u_ref[...]
    bu = _dot(u, bb_ref[...])
    re, im = bu[:, 0:S5_LANES], bu[:, S5_LANES:]
    row = _iota((C, 1), 0)
    for lvl in range(C.bit_length() - 1):
        sh = 1 << lvl
        ar, ai = pw_re_ref[lvl:lvl + 1, :], pw_im_ref[lvl:lvl + 1, :]
        sre = jnp.where(row >= sh, pltpu.roll(re, sh, 0), 0.0)
        sim = jnp.where(row >= sh, pltpu.roll(im, sh, 0), 0.0)
        re, im = re + ar * sre - ai * sim, im + ar * sim + ai * sre
    cr, ci = c_re_scr[0:1, :], c_im_scr[0:1, :]
    tr, tq = tab_re_ref[...], tab_im_ref[...]
    re, im = re + tr * cr - tq * ci, im + tr * ci + tq * cr
    c_re_scr[0:1, :] = re[C - 1:C, :]
    c_im_scr[0:1, :] = im[C - 1:C, :]
    y = _dot(jnp.concatenate([re, im], axis=1), cc_ref[...])
    y = jax.nn.gelu(y + d_ref[...] * u)
    o_ref[...] = y * jax.nn.sigmoid(_dot(y, gw_ref[...]) + gb_ref[...])


def _s5(z, bb, cc, pw_re, pw_im, tab_re, tab_im, d, gw, gb):
    B, Lp, _ = z.shape
    C = CHUNK
    const = lambda a: pl.BlockSpec(a.shape, lambda b, j: (0,) * a.ndim)
    consts = (bb, cc, pw_re, pw_im, tab_re, tab_im, d, gw, gb)
    return pl.pallas_call(
        _s5_kernel,
        grid=(B, Lp // C),
        in_specs=[pl.BlockSpec((None, C, GROUP_W), lambda b, j: (b, j, 0))] + [const(a) for a in consts],
        out_specs=pl.BlockSpec((None, C, GROUP_W), lambda b, j: (b, j, 0)),
        out_shape=jax.ShapeDtypeStruct((B, Lp, GROUP_W), F32),
        scratch_shapes=[pltpu.VMEM((8, S5_LANES), F32), pltpu.VMEM((8, S5_LANES), F32)],
        compiler_params=pltpu.CompilerParams(dimension_semantics=("parallel", "arbitrary")),
        name="s5",
    )(z, *consts)


def _rope_tables(lp):
    half = HEAD_DIM // 2
    pos = jnp.arange(lp) - PAD_FRONT
    inv_freq = ROPE_BASE ** (-jnp.arange(half, dtype=F32) / half)
    ang = pos.astype(F32)[:, None] * inv_freq[None, :]
    cos, sin = jnp.cos(ang), jnp.sin(ang)
    cos_t = jnp.tile(jnp.concatenate([cos, cos], axis=1), (1, N_HEADS))
    sin_t = jnp.tile(jnp.concatenate([-sin, sin], axis=1), (1, N_HEADS))
    return cos_t, sin_t


def _retention_tables():
    C = CHUNK
    log_g = jnp.log1p(-jnp.exp2(-5.0 - jnp.arange(N_HEADS, dtype=F32)))
    t = jnp.arange(C, dtype=F32)
    rel = t[:, None] - t[None, :]
    causal = rel >= 0
    dm = jnp.where(causal[None], jnp.exp(jnp.where(causal, rel, 0.0)[None] * log_g[:, None, None]), 0.0)
    lane_g = jnp.repeat(log_g, HEAD_DIM)
    qdec = jnp.exp((t + 1.0)[:, None] * lane_g[None, :])
    kdec_t = jnp.exp(lane_g[:, None] * (C - 1.0 - t)[None, :])
    gcm = jnp.broadcast_to(jnp.exp(C * lane_g)[:, None], (GROUP_W, GROUP_W))
    return dm, qdec, kdec_t, gcm


def _s5_tables(a_re, a_im, log_dt, b_re, b_im, c_re, c_im):
    C = CHUNK
    G = S5_GROUPS
    lam_re, lam_im = a_re.astype(F32), a_im.astype(F32)
    dt = jnp.exp(log_dt.astype(F32))[:, None]
    mag, ph = jnp.exp(lam_re * dt), lam_im * dt
    ab_re, ab_im = mag * jnp.cos(ph), mag * jnp.sin(ph)
    den = lam_re * lam_re + lam_im * lam_im
    nr, ni = ab_re - 1.0, ab_im
    zc_re = (nr * lam_re + ni * lam_im) / den
    zc_im = (ni * lam_re - nr * lam_im) / den
    b_re, b_im = b_re.astype(F32), b_im.astype(F32)
    bb_re = zc_re[..., None] * b_re - zc_im[..., None] * b_im
    bb_im = zc_re[..., None] * b_im + zc_im[..., None] * b_re
    eye = jnp.eye(G, dtype=F32)
    to_state = lambda m: jnp.einsum('gpc,gh->gchp', m, eye).reshape(GROUP_W, S5_LANES)
    bb = jnp.concatenate([to_state(bb_re), to_state(bb_im)], axis=1)
    from_state = lambda m: jnp.einsum('gcp,gh->hpgc', m.astype(F32), eye).reshape(S5_LANES, GROUP_W)
    cc = jnp.concatenate([from_state(c_re), -from_state(c_im)], axis=0)

    def powers(n):
        n = n.astype(F32)[:, None, None]
        m, p = jnp.exp(lam_re * dt * n), lam_im * dt * n
        return (m * jnp.cos(p)).reshape(-1, S5_LANES), (m * jnp.sin(p)).reshape(-1, S5_LANES)

    pw_re, pw_im = powers(2 ** jnp.arange(8))
    tab_re, tab_im = powers(jnp.arange(1, C + 1))
    return bb.astype(BF16), cc.astype(BF16), pw_re, pw_im, tab_re, tab_im


def _pick_tile(lp, candidates):
    for t in candidates:
        if lp % t == 0:
            return t
    return CHUNK


def kernel(x, meta_tokens, norm_mix_g, w_in, hgrn_lb_logits, hgrn_norm_g, rwkv_mu, rwkv_w0, rwkv_w_up,
           rwkv_a0, rwkv_a_up, rwkv_g_up, rwkv_k_k, rwkv_k_a, rwkv_r_k, rwkv_ln_w, rwkv_ln_b, rwkv_v0,
           rwkv_v_down, rwkv_v_up, s5_a_re, s5_a_im, s5_log_dt, s5_b_re, s5_b_im, s5_c_re, s5_c_im, s5_d,
           s5_glu_w, s5_glu_b, w_out, norm_ffn_g, w_ffn_up, w_ffn_down, norm_f_g):
    B, seq, _ = x.shape
    depth = w_in.shape[0]
    assert seq % CHUNK == 0
    lp = CHUNK + seq
    tm = _pick_tile(lp, (384, 256, 128))
    row2 = lambda a: a.astype(F32).reshape(1, -1)

    meta = jnp.broadcast_to(meta_tokens.astype(x.dtype)[None], (B, N_META, D_MODEL))
    h = jnp.concatenate([jnp.zeros((B, PAD_FRONT, D_MODEL), x.dtype), meta, x], axis=1)

    p = jax.nn.softmax(hgrn_lb_logits.astype(F32), axis=0)
    lower_bounds = jnp.cumsum(p, axis=0) - p[0]
    cos_t, sin_t = _rope_tables(lp)
    dm, qdec, kdec_t, gcm = _retention_tables()

    def pad_rows(a, n):
        return jnp.pad(a, ((0, n - a.shape[0]), (0, 0)))

    v_first = None
    for l in range(depth):
        w = w_in[l]
        w_pad = jnp.concatenate([w[:, :2048 + RWKV_COLS], jnp.zeros((D_MODEL, RWKV_COLS_PAD - RWKV_COLS), w.dtype),
                                 w[:, 2048 + RWKV_COLS:]], axis=1).astype(BF16)
        z_ret, z_hgrn, z_rwkv, z_s5 = _in_proj(h, row2(norm_mix_g[l]), w_pad, tm)

        o_ret = _retention(z_ret, cos_t, sin_t, dm, qdec, kdec_t, gcm)
        o_hgrn = _hgrn2(z_hgrn, row2(lower_bounds[l]), row2(jnp.tile(hgrn_norm_g[l], N_HEADS)))

        mu = jnp.pad(rwkv_mu[l].astype(F32), (0, RWKV_COLS_PAD - RWKV_COLS)).reshape(1, -1)
        w_up = pad_rows(rwkv_w_up[l], 128).astype(BF16)
        a_up = jnp.concatenate([jnp.zeros_like(rwkv_a_up[l]), rwkv_a_up[l]], axis=0).astype(BF16)
        g_up = pad_rows(rwkv_g_up[l], GROUP_W).astype(BF16)
        params = (mu, row2(rwkv_w0[l]), w_up, row2(rwkv_a0[l]), a_up, g_up, row2(rwkv_k_k[l]),
                  row2(rwkv_k_a[l]), row2(rwkv_r_k[l]), row2(rwkv_ln_w[l]), row2(rwkv_ln_b[l]))
        vmix = None
        if l > 0:
            v_down = jnp.pad(rwkv_v_down[l - 1], ((0, 0), (0, 128 - rwkv_v_down.shape[-1]))).astype(BF16)
            v_up = pad_rows(rwkv_v_up[l - 1], 128).astype(BF16)
            vmix = (row2(rwkv_v0[l - 1]), v_down, v_up)
        o_rwkv, v_first = _rwkv7(z_rwkv, v_first, params, vmix)

        tabs = _s5_tables(s5_a_re[l], s5_a_im[l], s5_log_dt[l], s5_b_re[l], s5_b_im[l], s5_c_re[l], s5_c_im[l])
        o_s5 = _s5(z_s5, *tabs, row2(s5_d[l]), s5_glu_w[l].astype(BF16), row2(s5_glu_b[l]))

        h = _out_ffn(h, (o_ret, o_hgrn, o_rwkv, o_s5), w_out[l].astype(BF16), row2(norm_ffn_g[l]),
                     w_ffn_up[l].astype(BF16), w_ffn_down[l].astype(BF16), row2(norm_f_g), tm,
                     final=(l == depth - 1))
    return h[:, CHUNK:]
```

```python
import functools

import jax
import jax.numpy as jnp
from jax import lax
from jax.experimental import pallas as pl
from jax.experimental.pallas import tpu as pltpu

F32 = jnp.float32
BF16 = jnp.bfloat16

D_MODEL = 1024
N_META = 16
HEAD_DIM = 64
GROUP_W = 256
N_HEADS = 4
D_FF = 4096
ROPE_BASE = 10000.0
NORM_EPS = 1e-6
RWKV_LN_EPS = 64e-5
S5_GROUPS = 16
S5_CH = 16
S5_STATE = 64
S5_LANES = S5_GROUPS * S5_STATE
RWKV_COLS = 1056
RWKV_COLS_PAD = 1152
N_Z = 1024 + 1024 + RWKV_COLS_PAD + 256

CHUNK = 128
PAD_FRONT = CHUNK - N_META
VMEM_LIMIT = 56 * 1024 * 1024
RET_ROWS, GLA_ROWS, RWKV_ROWS, S5_ROWS = 2, 2, 2, 2


def _rows_per_step(batch, rows):
    while batch % rows:
        rows -= 1
    return rows


def _dot(a, b):
    return jnp.dot(a.astype(BF16), b.astype(BF16), preferred_element_type=F32)


def _dot_f32(a, b):
    return jnp.dot(a, b, preferred_element_type=F32, precision=lax.Precision.HIGHEST)


def _dot_nt(a, b):
    return lax.dot_general(a.astype(BF16), b.astype(BF16), (((1,), (1,)), ((), ())),
                           preferred_element_type=F32)


def _split_dot(x, w_bf16):
    hi = x.astype(BF16)
    lo = (x - hi.astype(F32)).astype(BF16)
    return (jnp.dot(hi, w_bf16, preferred_element_type=F32)
            + jnp.dot(lo, w_bf16, preferred_element_type=F32))


def _iota(shape, dim):
    return lax.broadcasted_iota(jnp.int32, shape, dim)


def _head_masks():
    lane = _iota((1, GROUP_W), 1)
    return [((lane >> 6) == h).astype(F32) for h in range(N_HEADS)]


def _block_diag_mask():
    return (_iota((GROUP_W, GROUP_W), 0) >> 6) == (_iota((GROUP_W, GROUP_W), 1) >> 6)


def _head_sum(x, bd_ones):
    return _split_dot(x, bd_ones)


def _silu(x):
    return x * jax.nn.sigmoid(x)


def _per_row(body, rows, n_row_in, n_in, n_out):
    def kern(*refs):
        ins, outs, scrs = refs[:n_in], refs[n_in:n_in + n_out], refs[n_in + n_out:]

        @pl.when(pl.program_id(1) == 0)
        def _():
            for s in scrs:
                s[...] = jnp.zeros_like(s)

        live = [body(*[r.at[bi] for r in ins[:n_row_in]], *ins[n_row_in:], *[r.at[bi] for r in outs],
                     *[s.at[bi] for s in scrs]) for bi in range(rows)]
        while live:
            for g in list(live):
                if next(g, "done") == "done":
                    live.remove(g)
    return kern


def _row_spec(rows, n):
    return pl.BlockSpec((rows, CHUNK, n), lambda b, j: (b, j, 0))


def _const_spec(a):
    return pl.BlockSpec(a.shape, lambda b, j: (0,) * a.ndim)


def _in_proj_kernel(x_ref, g_ref, w_ref, zr_ref, zh_ref, zw_ref, zs_ref, *, tm):
    x = x_ref[...]
    ms = jnp.mean(x * x, axis=-1, keepdims=True)
    xn = x * lax.rsqrt(ms + NORM_EPS) * g_ref[...]
    row = pl.program_id(1) * tm + _iota((tm, 1), 0)
    xb = jnp.where(row < PAD_FRONT, 0.0, xn).astype(BF16)
    zr_ref[...] = jnp.dot(xb, w_ref[:, 0:1024], preferred_element_type=F32)
    zh_ref[...] = jnp.dot(xb, w_ref[:, 1024:2048], preferred_element_type=F32)
    zw_ref[...] = jnp.dot(xb, w_ref[:, 2048:2048 + RWKV_COLS_PAD], preferred_element_type=F32)
    zs_ref[...] = jnp.dot(xb, w_ref[:, 2048 + RWKV_COLS_PAD:N_Z], preferred_element_type=F32)


def _in_proj(h, g, w, tm):
    B, Lp, _ = h.shape
    row_spec = lambda n: pl.BlockSpec((None, tm, n), lambda b, j: (b, j, 0))
    full = lambda shape: pl.BlockSpec(shape, lambda b, j: (0,) * len(shape))
    return pl.pallas_call(
        functools.partial(_in_proj_kernel, tm=tm),
        grid=(B, Lp // tm),
        in_specs=[row_spec(D_MODEL), full((1, D_MODEL)), full((D_MODEL, N_Z))],
        out_specs=[row_spec(1024), row_spec(1024), row_spec(RWKV_COLS_PAD), row_spec(256)],
        out_shape=[jax.ShapeDtypeStruct((B, Lp, n), F32) for n in (1024, 1024, RWKV_COLS_PAD, 256)],
        compiler_params=pltpu.CompilerParams(dimension_semantics=("parallel", "parallel"),
                                             vmem_limit_bytes=VMEM_LIMIT),
        name="in_proj",
    )(h, g, w)


def _ffn_kernel(h_ref, o1_ref, o2_ref, o3_ref, o4_ref, wo_ref, g_ref, wu_ref, wd_ref, gf_ref, out_ref,
                *, final, fc):
    o = jnp.concatenate([o1_ref[...], o2_ref[...], o3_ref[...], o4_ref[...]], axis=1).astype(BF16)
    acc = h_ref[...] + jnp.dot(o, wo_ref[...], preferred_element_type=F32)
    ms = jnp.mean(acc * acc, axis=-1, keepdims=True)
    hb = (acc * lax.rsqrt(ms + NORM_EPS) * g_ref[...]).astype(BF16)
    mlp = None
    for c in range(0, D_FF, fc):
        u = jnp.dot(hb, wu_ref[:, c:c + fc], preferred_element_type=F32)
        u = jnp.square(jnp.maximum(u, 0.0)).astype(BF16)
        d = jnp.dot(u, wd_ref[c:c + fc, :], preferred_element_type=F32)
        mlp = d if mlp is None else mlp + d
    y = acc + mlp
    if final:
        ms = jnp.mean(y * y, axis=-1, keepdims=True)
        y = y * lax.rsqrt(ms + NORM_EPS) * gf_ref[...]
    out_ref[...] = y


def _out_ffn(h, outs, wo, g, wu, wd, gf, tm, final):
    B, Lp, _ = h.shape
    row_spec = lambda n: pl.BlockSpec((None, tm, n), lambda b, j: (b, j, 0))
    const = lambda shape: pl.BlockSpec(shape, lambda b, j: (0,) * len(shape))
    return pl.pallas_call(
        functools.partial(_ffn_kernel, final=final, fc=1024),
        grid=(B, Lp // tm),
        in_specs=[row_spec(D_MODEL)] + [row_spec(GROUP_W)] * 4 + [
            const((D_MODEL, D_MODEL)), const((1, D_MODEL)), const((D_MODEL, D_FF)),
            const((D_FF, D_MODEL)), const((1, D_MODEL))],
        out_specs=row_spec(D_MODEL),
        out_shape=jax.ShapeDtypeStruct((B, Lp, D_MODEL), F32),
        compiler_params=pltpu.CompilerParams(dimension_semantics=("parallel", "parallel"),
                                             vmem_limit_bytes=VMEM_LIMIT),
        name="out_ffn",
    )(h, *outs, wo, g, wu, wd, gf)


def _ret_kernel(z_ref, cos_ref, sin_ref, dm_ref, qdec_ref, kdec_t_ref, gcm_ref, o_ref, s_scr):
    hm = _head_masks()
    bdm = _block_diag_mask()
    bd_ones = bdm.astype(BF16)
    z = z_ref[...]
    q, k, v, g = z[:, 0:256], z[:, 256:512], z[:, 512:768], z[:, 768:1024]
    cos, sin = cos_ref[...], sin_ref[...]
    upper = (_iota((1, GROUP_W), 1) & (HEAD_DIM - 1)) >= HEAD_DIM // 2

    def rot(t):
        swapped = jnp.where(upper, pltpu.roll(t, HEAD_DIM // 2, 1),
                            pltpu.roll(t, GROUP_W - HEAD_DIM // 2, 1))
        return t * cos + swapped * sin

    qr = rot(q)
    kr = rot(k) * HEAD_DIM ** -0.5
    yield
    s = s_scr[...]
    o = _dot(qr * qdec_ref[...], s)
    vb = v.astype(BF16)
    sc = _dot_nt(jnp.concatenate([qr * hm[h] for h in range(N_HEADS)], axis=0), kr)
    kv = _dot(kr.T * kdec_t_ref[...], vb)
    s_scr[...] = gcm_ref[...] * s + jnp.where(bdm, kv, 0.0)
    yield
    C = CHUNK
    for h in range(N_HEADS):
        p = (sc[h * C:(h + 1) * C] * dm_ref[h]).astype(BF16)
        o = o + jnp.dot(p, vb, preferred_element_type=F32) * hm[h]
    yield
    msq = _head_sum(o * o, bd_ones) * (1.0 / HEAD_DIM)
    o_ref[...] = o * lax.rsqrt(msq + NORM_EPS) * _silu(g)


def _retention(z, cos, sin, dm, qdec, kdec_t, gcm):
    B, Lp, _ = z.shape
    C = CHUNK
    rows = _rows_per_step(B, RET_ROWS)
    pos_spec = pl.BlockSpec((C, GROUP_W), lambda b, j: (j, 0))
    return pl.pallas_call(
        _per_row(_ret_kernel, rows, 1, 7, 1),
        grid=(B // rows, Lp // C),
        in_specs=[_row_spec(rows, 1024), pos_spec, pos_spec,
                  _const_spec(dm), _const_spec(qdec), _const_spec(kdec_t), _const_spec(gcm)],
        out_specs=_row_spec(rows, GROUP_W),
        out_shape=jax.ShapeDtypeStruct((B, Lp, GROUP_W), F32),
        scratch_shapes=[pltpu.VMEM((rows, GROUP_W, GROUP_W), F32)],
        compiler_params=pltpu.CompilerParams(dimension_semantics=("parallel", "arbitrary")),
        name="retention",
    )(z, cos, sin, dm, qdec, kdec_t, gcm)


def _gla_kernel(z_ref, lb_ref, ng_ref, o_ref, s_scr):
    C = CHUNK
    hm = _head_masks()
    bdm = _block_diag_mask()
    bd_ones = bdm.astype(BF16)
    z = z_ref[...]
    q, f, v, g =z[:, 0:256], z[:, 256:512], z[:, 512:768], z[:, 768:1024]
    lb = lb_ref[...]
    forget = lb + (1.0 - lb) * jax.nn.sigmoid(f)
    k = 1.0 - forget
    logf = jnp.log(forget)
    q = _silu(q) * HEAD_DIM ** -0.5
    row = _iota((C, 1), 0)
    ti, si = _iota((C, C), 0), _iota((C, C), 1)
    b = _dot_f32((ti >= si).astype(F32), logf)
    yield

    terms, shifted_v = [q * k], [v]
    for d in range(1, 8):
        valid = (row & 7) >= d
        diff = jnp.where(valid, b - pltpu.roll(b, d, 0), 0.0)
        terms.append(jnp.where(valid, q * pltpu.roll(k, d, 0) * jnp.exp(diff), 0.0))
        shifted_v.append(pltpu.roll(v, d, 0))
    band = _head_sum(jnp.concatenate(terms, axis=0), bd_ones)
    yield
    o = band[0:C] * shifted_v[0]
    for d in range(1, 8):
        o = o + band[d * C:(d + 1) * C] * shifted_v[d]

    p = jnp.zeros((N_HEADS * C, C), F32)
    m = 8
    while 2 * m <= C:
        b3 = b.reshape(C // (2 * m), 2 * m, GROUP_W)
        ref = b3[:, m - 1:m, :]
        dq = jnp.minimum(b3 - ref, 0.0).reshape(C, GROUP_W)
        dk = jnp.minimum(ref - b3, 0.0).reshape(C, GROUP_W)
        second = (row & (2 * m - 1)) >= m
        qm = jnp.where(second, q * jnp.exp(dq), 0.0)
        km = jnp.where(second, 0.0, k * jnp.exp(dk))
        lq = jnp.concatenate([qm * hm[h] for h in range(N_HEADS)], axis=0)
        sc = _dot(lq, km.T)
        same = (ti // (2 * m)) == (si // (2 * m))
        p = p + jnp.where(jnp.concatenate([same] * N_HEADS, axis=0), sc, 0.0)
        m *= 2
        yield
    vb = v.astype(BF16)
    for h in range(N_HEADS):
        o = o + jnp.dot(p[h * C:(h + 1) * C].astype(BF16), vb, preferred_element_type=F32) * hm[h]
    yield

    s = s_scr[...]
    o = o + _dot(q * jnp.exp(b), s)
    b_t = b.T
    b_last = b_t[:, C - 1:C]
    k_end = k.T * jnp.exp(b_last - b_t)
    s_scr[...] = jnp.exp(b_last) * s + jnp.where(bdm, _dot(k_end, vb), 0.0)
    yield
    msq = _head_sum(o * o, bd_ones) * (1.0 / HEAD_DIM)
    o_ref[...] = o * lax.rsqrt(msq + NORM_EPS) * ng_ref[...] * _silu(g)


def _hgrn2(z, lb, ng):
    B, Lp, _ = z.shape
    C = CHUNK
    rows = _rows_per_step(B, GLA_ROWS)
    return pl.pallas_call(
        _per_row(_gla_kernel, rows, 1, 3, 1),
        grid=(B // rows, Lp // C),
        in_specs=[_row_spec(rows, 1024), _const_spec(lb), _const_spec(ng)],
        out_specs=_row_spec(rows, GROUP_W),
        out_shape=jax.ShapeDtypeStruct((B, Lp, GROUP_W), F32),
        scratch_shapes=[pltpu.VMEM((rows, GROUP_W, GROUP_W), F32)],
        compiler_params=pltpu.CompilerParams(dimension_semantics=("parallel", "arbitrary")),
        name="hgrn2",
    )(z, lb, ng)


def _rwkv_kernel(*refs, has_vmix):
    C = CHUNK
    if has_vmix:
        (z_ref, vf_ref, mu_ref, w0_ref, wup_ref, a0_ref, aup_ref, gup_ref, kk_ref, ka_ref, rk_ref,
         lnw_ref, lnb_ref, v0_ref, vdn_ref, vup_ref, o_ref, t_scr, carry_scr) = refs
    else:
        (z_ref, mu_ref, w0_ref, wup_ref, a0_ref, aup_ref, gup_ref, kk_ref, ka_ref, rk_ref,
         lnw_ref, lnb_ref, o_ref, vf_out_ref, t_scr, carry_scr) = refs

    hm = _head_masks()
    bdm = _block_diag_mask()
    bd_ones = bdm.astype(BF16)
    row = _iota((C, 1), 0)
    ti, si = _iota((C, C), 0), _iota((C, C), 1)

    z = z_ref[...]
    z_prev = jnp.where(row == 0, carry_scr[0:1, :], pltpu.roll(z, 1, 0))
    carry_scr[0:1, :] = z[C - 1:C, :]
    zf = z + (z_prev - z) * mu_ref[...]
    r, k, v = zf[:, 0:256], zf[:, 256:512], zf[:, 512:768]
    wa, gl = zf[:, 768:896], zf[:, 896:RWKV_COLS_PAD]

    w_log = -jax.nn.softplus(-(w0_ref[...] + _dot(jnp.tanh(wa), wup_ref[...]))) - 0.5
    logw = -jnp.exp(w_log)
    a = jax.nn.sigmoid(a0_ref[...] + _dot(wa, aup_ref[...]))
    gate = _dot(jax.nn.sigmoid(gl), gup_ref[...])
    if has_vmix:
        mix = jax.nn.sigmoid(v0_ref[...] + _dot(_dot(v, vdn_ref[...]), vup_ref[...]))
        v = v + (vf_ref[...] - v) * mix
    else:
        vf_out_ref[...] = v
    kk = k * kk_ref[...]
    kk = kk / jnp.maximum(jnp.sqrt(_head_sum(kk * kk, bd_ones)), 1e-12)
    k = k * (1.0 + (a - 1.0) * ka_ref[...])
    alpha = -kk * a
    yield

    gc = _dot_f32((ti >= si).astype(F32), logw)
    yield
    gp = gc - logw
    gm = gc[C // 2:C // 2 + 1, :]
    r_mid, b_mid = r * jnp.exp(gc - gm), kk * jnp.exp(gp - gm)
    r_abs, b_abs = r * jnp.exp(gc), kk * jnp.exp(gp)
    g_t, a_t, k_t = gc.T, alpha.T, k.T
    from_mid = jnp.exp(g_t[:, C // 2:C // 2 + 1] - g_t)
    g_last = g_t[:, C - 1:C]
    to_end = jnp.exp(g_last - g_t)

    yield
    head_rows = [(_iota((GROUP_W, 1), 0) >> 6) == h for h in range(N_HEADS)]
    a_mid, k_mid = a_t * from_mid, k_t * from_mid
    rhs_t = jnp.concatenate([jnp.where(m, a_mid, 0.0) for m in head_rows]
                            + [jnp.where(m, k_mid, 0.0) for m in head_rows], axis=1)
    sc = _dot(jnp.concatenate([b_mid, r_mid], axis=0), rhs_t)
    yield

    t0 = t_scr[...]
    t0b = t0.astype(BF16)
    strict2 = jnp.concatenate([ti > si] * 2, axis=1)
    incl2 = jnp.concatenate([ti >= si] * 2, axis=1)
    first_half = _iota((1, 2 * C), 1) < C
    pairs = range(N_HEADS // 2)

    def stack(x, p):
        return jnp.concatenate([x * hm[2 * p], x * hm[2 * p + 1]], axis=0).astype(BF16)

    def block_diag(xab):
        return jnp.concatenate([jnp.where(first_half, xab, 0.0), jnp.where(first_half, 0.0, xab)],
                               axis=0).astype(BF16)

    v_stack = [stack(v, p) for p in pairs]
    rhs = jnp.dot(b_abs.astype(BF16), t0b, preferred_element_type=F32)
    for p in pairs:
        bm = jnp.where(strict2, sc[0:C, (N_HEADS + 2 * p) * C:(N_HEADS + 2 * p + 2) * C], 0.0)
        rhs = rhs + jnp.dot(bm.astype(BF16), v_stack[p], preferred_element_type=F32)
    yield

    n_ab = [jnp.where(strict2, sc[0:C, 2 * p * C:(2 * p + 2) * C], 0.0) for p in pairs]
    p_ab = [jnp.dot(n_ab[p].astype(BF16), block_diag(n_ab[p]), preferred_element_type=F32) for p in pairs]
    yield
    levels = C.bit_length() - 1
    for lvl in range(1, levels):
        last = lvl == levels - 1
        for p in pairs:
            lhs = n_ab[p] if last else jnp.concatenate([p_ab[p], n_ab[p]], axis=0)
            res = jnp.dot(lhs.astype(BF16), block_diag(p_ab[p]), preferred_element_type=F32)
            n_ab[p] = n_ab[p] + p_ab[p] + (res if last else res[C:2 * C])
            if not last:
                p_ab[p] = res[0:C]
        yield

    u = rhs + sum(jnp.dot(n_ab[p].astype(BF16), stack(rhs, p), preferred_element_type=F32) for p in pairs)
    yield
    o = jnp.dot(r_abs.astype(BF16), t0b, preferred_element_type=F32)
    for p in pairs:
        ao = jnp.where(incl2, sc[C:2 * C, 2 * p * C:(2 * p + 2) * C], 0.0)
        bo = jnp.where(incl2, sc[C:2 * C, (N_HEADS + 2 * p) * C:(N_HEADS + 2 * p + 2) * C], 0.0)
        o = o + jnp.dot(jnp.concatenate([ao, bo], axis=1).astype(BF16),
                        jnp.concatenate([stack(u, p), v_stack[p]], axis=0), preferred_element_type=F32)
    uv = jnp.concatenate([u, v], axis=0).astype(BF16)
    upd = _dot(jnp.concatenate([a_t * to_end, k_t * to_end], axis=1), uv)
    t_scr[...] = jnp.exp(g_last) * t0 + jnp.where(bdm, upd, 0.0)
    yield

    inv_d = 1.0 / HEAD_DIM
    dev = o - _head_sum(o, bd_ones) * inv_d
    var = _head_sum(dev * dev, bd_ones) * inv_d
    on = dev * lax.rsqrt(var + RWKV_LN_EPS) * lnw_ref[...] + lnb_ref[...]
    on = on + _head_sum(r * k * rk_ref[...], bd_ones) * v
    o_ref[...] = on * gate


def _rwkv7(z, v_first, params, vmix):
    B, Lp, _ = z.shape
    C = CHUNK
    has_vmix = vmix is not None
    rows = _rows_per_step(B, RWKV_ROWS)
    tok = lambda n: _row_spec(rows, n)
    args, specs = [z], [tok(RWKV_COLS_PAD)]
    if has_vmix:
        args.append(v_first)
        specs.append(tok(GROUP_W))
    n_row_in = len(args)
    extra = list(params) + (list(vmix) if has_vmix else [])
    args += extra
    specs += [_const_spec(a) for a in extra]
    o_shape = jax.ShapeDtypeStruct((B, Lp, GROUP_W), F32)
    n_out = 1 if has_vmix else 2
    res = pl.pallas_call(
        _per_row(functools.partial(_rwkv_kernel, has_vmix=has_vmix), rows, n_row_in, len(args), n_out),
        grid=(B // rows, Lp // C),
        in_specs=specs,
        out_specs=tok(GROUP_W) if has_vmix else [tok(GROUP_W), tok(GROUP_W)],
        out_shape=o_shape if has_vmix else [o_shape, o_shape],
        scratch_shapes=[pltpu.VMEM((rows, GROUP_W, GROUP_W), F32), pltpu.VMEM((rows, 8, RWKV_COLS_PAD), F32)],
        compiler_params=pltpu.CompilerParams(dimension_semantics=("parallel", "arbitrary")),
        name="rwkv7",
    )(*args)
    return (res, v_first) if has_vmix else (res[0], res[1])


def _s5_kernel(u_ref, bb_ref, cc_ref, pw_re_ref, pw_im_ref, tab_re_ref, tab_im_ref, d_ref, gw_ref, gb_ref,
               o_ref, c_re_scr, c_im_scr):
    C = CHUNK
    u = u_ref[...]
    bu = _dot(u, bb_ref[...])
    yield
    re, im = bu[:, 0:S5_LANES], bu[:, S5_LANES:]
    row = _iota((C, 1), 0)
    for lvl in range(C.bit_length() - 1):
        sh = 1 << lvl
        ar, ai = pw_re_ref[lvl:lvl + 1, :], pw_im_ref[lvl:lvl + 1, :]
        sre = jnp.where(row >= sh, pltpu.roll(re, sh, 0), 0.0)
        sim = jnp.where(row >= sh, pltpu.roll(im, sh, 0), 0.0)
        re, im = re + ar * sre - ai * sim, im + ar * sim + ai * sre
    cr, ci = c_re_scr[0:1, :], c_im_scr[0:1, :]
    tr, tq = tab_re_ref[...], tab_im_ref[...]
    re, im = re + tr * cr - tq * ci, im + tr * ci + tq * cr
    c_re_scr[0:1, :] = re[C - 1:C, :]
    c_im_scr[0:1, :] = im[C - 1:C, :]
    y = _dot(jnp.concatenate([re, im], axis=1), cc_ref[...])
    y = jax.nn.gelu(y + d_ref[...] * u)
    o_ref[...] = y * jax.nn.sigmoid(_dot(y, gw_ref[...]) + gb_ref[...])


def _s5(z, bb, cc, pw_re, pw_im, tab_re, tab_im, d, gw, gb):
    B, Lp, _ = z.shape
    C = CHUNK
    rows = _rows_per_step(B, S5_ROWS)
    consts = (bb, cc, pw_re, pw_im, tab_re, tab_im, d, gw, gb)
    return pl.pallas_call(
        _per_row(_s5_kernel, rows, 1, 1 + len(consts), 1),
        grid=(B // rows, Lp // C),
        in_specs=[_row_spec(rows, GROUP_W)] + [_const_spec(a) for a in consts],
        out_specs=_row_spec(rows, GROUP_W),
        out_shape=jax.ShapeDtypeStruct((B, Lp, GROUP_W), F32),
        scratch_shapes=[pltpu.VMEM((rows, 8, S5_LANES), F32), pltpu.VMEM((rows, 8, S5_LANES), F32)],
        compiler_params=pltpu.CompilerParams(dimension_semantics=("parallel", "arbitrary")),
        name="s5",
    )(z, *consts)


def _rope_tables(lp):
    half = HEAD_DIM // 2
    pos = jnp.arange(lp) - PAD_FRONT
    inv_freq = ROPE_BASE ** (-jnp.arange(half, dtype=F32) / half)
    ang = pos.astype(F32)[:, None] * inv_freq[None, :]
    cos, sin = jnp.cos(ang), jnp.sin(ang)
    cos_t = jnp.tile(jnp.concatenate([cos, cos], axis=1), (1, N_HEADS))
    sin_t = jnp.tile(jnp.concatenate([-sin, sin], axis=1), (1, N_HEADS))
    return cos_t, sin_t


def _retention_tables():
    C = CHUNK
    log_g = jnp.log1p(-jnp.exp2(-5.0 - jnp.arange(N_HEADS, dtype=F32)))
    t = jnp.arange(C, dtype=F32)
    rel = t[:, None] - t[None, :]
    causal = rel >= 0
    dm = jnp.where(causal[None], jnp.exp(jnp.where(causal, rel, 0.0)[None] * log_g[:, None, None]), 0.0)
    lane_g = jnp.repeat(log_g, HEAD_DIM)
    qdec = jnp.exp((t + 1.0)[:, None] * lane_g[None, :])
    kdec_t = jnp.exp(lane_g[:, None] * (C - 1.0 - t)[None, :])
    gcm = jnp.broadcast_to(jnp.exp(C * lane_g)[:, None], (GROUP_W, GROUP_W))
    return dm, qdec, kdec_t, gcm


def _s5_tables(a_re, a_im, log_dt, b_re, b_im, c_re, c_im):
    C = CHUNK
    G = S5_GROUPS
    lam_re, lam_im = a_re.astype(F32), a_im.astype(F32)
    dt = jnp.exp(log_dt.astype(F32))[:, None]
    mag, ph = jnp.exp(lam_re * dt), lam_im * dt
    ab_re, ab_im = mag * jnp.cos(ph), mag * jnp.sin(ph)
    den = lam_re * lam_re + lam_im * lam_im
    nr, ni = ab_re - 1.0, ab_im
    zc_re = (nr * lam_re + ni * lam_im) / den
    zc_im = (ni * lam_re - nr * lam_im) / den
    b_re, b_im = b_re.astype(F32), b_im.astype(F32)
    bb_re = zc_re[..., None] * b_re - zc_im[..., None] * b_im
    bb_im = zc_re[..., None] * b_im + zc_im[..., None] * b_re
    eye = jnp.eye(G, dtype=F32)
    to_state = lambda m: jnp.einsum('gpc,gh->gchp', m, eye).reshape(GROUP_W, S5_LANES)
    bb = jnp.concatenate([to_state(bb_re), to_state(bb_im)], axis=1)
    from_state = lambda m: jnp.einsum('gcp,gh->hpgc', m.astype(F32), eye).reshape(S5_LANES, GROUP_W)
    cc = jnp.concatenate([from_state(c_re), -from_state(c_im)], axis=0)

    def powers(n):
        n = n.astype(F32)[:, None, None]
        m, p = jnp.exp(lam_re * dt * n), lam_im * dt * n
        return (m * jnp.cos(p)).reshape(-1, S5_LANES), (m * jnp.sin(p)).reshape(-1, S5_LANES)

    pw_re, pw_im = powers(2 ** jnp.arange(8))
    tab_re, tab_im = powers(jnp.arange(1, C + 1))
    return bb.astype(BF16), cc.astype(BF16), pw_re, pw_im, tab_re, tab_im


def _pick_tile(lp, candidates):
    for t in candidates:
        if lp % t == 0:
            return t
    return CHUNK


def kernel(x, meta_tokens, norm_mix_g, w_in, hgrn_lb_logits, hgrn_norm_g, rwkv_mu, rwkv_w0, rwkv_w_up,
           rwkv_a0, rwkv_a_up, rwkv_g_up, rwkv_k_k, rwkv_k_a, rwkv_r_k, rwkv_ln_w, rwkv_ln_b, rwkv_v0,
           rwkv_v_down, rwkv_v_up, s5_a_re, s5_a_im, s5_log_dt, s5_b_re, s5_b_im, s5_c_re, s5_c_im, s5_d,
           s5_glu_w, s5_glu_b, w_out, norm_ffn_g, w_ffn_up, w_ffn_down, norm_f_g):
    B, seq, _ = x.shape
    depth = w_in.shape[0]
    assert seq % CHUNK == 0
    lp = CHUNK + seq
    tm = _pick_tile(lp, (384, 256, 128))
    row2 = lambda a: a.astype(F32).reshape(1, -1)

    meta = jnp.broadcast_to(meta_tokens.astype(x.dtype)[None], (B, N_META, D_MODEL))
    h = jnp.concatenate([jnp.zeros((B, PAD_FRONT, D_MODEL), x.dtype), meta, x], axis=1)

    p = jax.nn.softmax(hgrn_lb_logits.astype(F32), axis=0)
    lower_bounds = jnp.cumsum(p, axis=0) - p[0]
    cos_t, sin_t = _rope_tables(lp)
    dm, qdec, kdec_t, gcm = _retention_tables()

    def pad_rows(a, n):
        return jnp.pad(a, ((0, n - a.shape[0]), (0, 0)))

    v_first = None
    for l in range(depth):
        w = w_in[l]
        w_pad = jnp.concatenate([w[:, :2048 + RWKV_COLS], jnp.zeros((D_MODEL, RWKV_COLS_PAD - RWKV_COLS), w.dtype),
                                 w[:, 2048 + RWKV_COLS:]], axis=1).astype(BF16)
        z_ret, z_hgrn, z_rwkv, z_s5 = _in_proj(h, row2(norm_mix_g[l]), w_pad, tm)

        o_ret = _retention(z_ret, cos_t, sin_t, dm, qdec, kdec_t, gcm)
        o_hgrn = _hgrn2(z_hgrn, row2(lower_bounds[l]), row2(jnp.tile(hgrn_norm_g[l], N_HEADS)))

        mu = jnp.pad(rwkv_mu[l].astype(F32), (0, RWKV_COLS_PAD - RWKV_COLS)).reshape(1, -1)
        w_up = pad_rows(rwkv_w_up[l], 128).astype(BF16)
        a_up = jnp.concatenate([jnp.zeros_like(rwkv_a_up[l]), rwkv_a_up[l]], axis=0).astype(BF16)
        g_up = pad_rows(rwkv_g_up[l], GROUP_W).astype(BF16)
        params = (mu, row2(rwkv_w0[l]), w_up, row2(rwkv_a0[l]), a_up, g_up, row2(rwkv_k_k[l]),
                  row2(rwkv_k_a[l]), row2(rwkv_r_k[l]), row2(rwkv_ln_w[l]), row2(rwkv_ln_b[l]))
        vmix = None
        if l > 0:
            v_down = jnp.pad(rwkv_v_down[l - 1], ((0, 0), (0, 128 - rwkv_v_down.shape[-1]))).astype(BF16)
            v_up = pad_rows(rwkv_v_up[l - 1], 128).astype(BF16)
            vmix = (row2(rwkv_v0[l - 1]), v_down, v_up)
        o_rwkv, v_first = _rwkv7(z_rwkv, v_first, params, vmix)

        tabs = _s5_tables(s5_a_re[l], s5_a_im[l], s5_log_dt[l], s5_b_re[l], s5_b_im[l], s5_c_re[l], s5_c_im[l])
        o_s5 = _s5(z_s5, *tabs, row2(s5_d[l]), s5_glu_w[l].astype(BF16), row2(s5_glu_b[l]))

        h = _out_ffn(h, (o_ret, o_hgrn, o_rwkv, o_s5), w_out[l].astype(BF16), row2(norm_ffn_g[l]),
                     w_ffn_up[l].astype(BF16), w_ffn_down[l].astype(BF16), row2(norm_f_g), tm,
                     final=(l == depth - 1))
    return h[:, CHUNK:]
```

```python
import functools

import jax
import jax.numpy as jnp
from jax import lax
from jax.experimental import pallas as pl
from jax.experimental.pallas import tpu as pltpu

F32 = jnp.float32
BF16 = jnp.bfloat16

D_MODEL = 1024
N_META = 16
HEAD_DIM = 64
GROUP_W = 256
N_HEADS = 4
D_FF = 4096
ROPE_BASE = 10000.0
NORM_EPS = 1e-6
RWKV_LN_EPS = 64e-5
S5_GROUPS = 16
S5_CH = 16
S5_STATE = 64
S5_LANES = S5_GROUPS * S5_STATE
RWKV_COLS = 1056
RWKV_COLS_PAD = 1152
N_Z = 1024 + 1024 + RWKV_COLS_PAD + 256

CHUNK = 128
PAD_FRONT = CHUNK - N_META
VMEM_LIMIT = 56 * 1024 * 1024
RET_ROWS, GLA_ROWS, RWKV_ROWS, S5_ROWS = 2, 2, 2, 2
S5_BLOCK = 8


def _rows_per_step(batch, rows):
    while batch % rows:
        rows -= 1
    return rows


def _dot(a, b):
    return jnp.dot(a.astype(BF16), b.astype(BF16), preferred_element_type=F32)


def _dot_f32(a, b):
    return jnp.dot(a, b, preferred_element_type=F32, precision=lax.Precision.HIGHEST)


def _dot_nt(a, b):
    return lax.dot_general(a.astype(BF16), b.astype(BF16), (((1,), (1,)), ((), ())),
                           preferred_element_type=F32)


def _split_dot(x, w_bf16):
    hi = x.astype(BF16)
    lo = (x - hi.astype(F32)).astype(BF16)
    return (jnp.dot(hi, w_bf16, preferred_element_type=F32)
            + jnp.dot(lo, w_bf16, preferred_element_type=F32))


def _iota(shape, dim):
    return lax.broadcasted_iota(jnp.int32, shape, dim)


def _head_masks():
    lane = _iota((1, GROUP_W), 1)
    return [((lane >> 6) == h).astype(F32) for h in range(N_HEADS)]


def _block_diag_mask():
    return (_iota((GROUP_W, GROUP_W), 0) >> 6) == (_iota((GROUP_W, GROUP_W), 1) >> 6)


def _head_sum(x, bd_ones):
    return _split_dot(x, bd_ones)


def _silu(x):
    return x * jax.nn.sigmoid(x)


def _per_row(body, rows, n_row_in, n_in, n_out):
    def kern(*refs):
        ins, outs, scrs = refs[:n_in], refs[n_in:n_in + n_out], refs[n_in + n_out:]

        @pl.when(pl.program_id(1) == 0)
        def _():
            for s in scrs:
                s[...] = jnp.zeros_like(s)

        live = [body(*[r.at[bi] for r in ins[:n_row_in]], *ins[n_row_in:], *[r.at[bi] for r in outs],
                     *[s.at[bi] for s in scrs]) for bi in range(rows)]
        while live:
            for g in list(live):
                if next(g, "done") == "done":
                    live.remove(g)
    return kern


def _row_spec(rows, n):
    return pl.BlockSpec((rows, CHUNK, n), lambda b, j: (b, j, 0))


def _const_spec(a):
    return pl.BlockSpec(a.shape, lambda b, j: (0,) * a.ndim)


def _elem_spec(tm, n, offset):
    return pl.BlockSpec((pl.Element(1), pl.Element(tm), pl.Element(n)),
                        lambda b, j: (b, pl.multiple_of(offset(j), 128), 0))


def _stream_specs(first, tm):
    if not first:
        return [pl.BlockSpec((None, tm, D_MODEL), lambda b, j: (b, j, 0))]
    return [pl.BlockSpec((None, tm, D_MODEL), lambda b, j: (b, 0, 0)),
            _elem_spec(tm, D_MODEL, lambda j: jnp.maximum(j * tm - CHUNK, 0))]


def _read_stream(first, refs):
    if not first:
        return refs[0][...], refs[1:]
    return jnp.where(pl.program_id(1) == 0, refs[0][...], refs[1][0]), refs[2:]


def _resident(shape):
    return pl.BlockSpec(shape, lambda b, j: (0,) * len(shape), pipeline_mode=pl.Buffered(1))


def _in_proj_kernel(*refs, tm, first):
    x, (g_ref, w_ref, zr_ref, zh_ref, zw_ref, zsa_ref, zsb_ref) = _read_stream(first, refs)
    ms = jnp.mean(x * x, axis=-1, keepdims=True)
    xn = x * lax.rsqrt(ms + NORM_EPS) * g_ref[...]
    row = pl.program_id(1) * tm + _iota((tm, 1), 0)
    xb = jnp.where(row < PAD_FRONT, 0.0, xn).astype(BF16)
    zr_ref[...] = jnp.dot(xb, w_ref[:, 0:1024], preferred_element_type=F32)
    zh_ref[...] = jnp.dot(xb, w_ref[:, 1024:2048], preferred_element_type=F32)
    zw_ref[...] = jnp.dot(xb, w_ref[:, 2048:2048 + RWKV_COLS_PAD], preferred_element_type=F32)
    zs = jnp.dot(xb, w_ref[:, 2048 + RWKV_COLS_PAD:N_Z], preferred_element_type=F32)
    zsa_ref[...] = zs[:, 0:128]
    zsb_ref[...] = zs[:, 128:256]


def _in_proj(stream, lp, g, w, tm):
    first = len(stream) == 2
    B = stream[0].shape[0]
    row_spec = lambda n: pl.BlockSpec((None, tm, n), lambda b, j: (b, j, 0))
    widths = (1024, 1024, RWKV_COLS_PAD, 128, 128)
    return pl.pallas_call(
        functools.partial(_in_proj_kernel, tm=tm, first=first),
        grid=(B, lp // tm),
        in_specs=_stream_specs(first, tm) + [_resident((1, D_MODEL)), _resident((D_MODEL, N_Z))],
        out_specs=[row_spec(n) for n in widths],
        out_shape=[jax.ShapeDtypeStruct((B, lp, n), F32) for n in widths],
        compiler_params=pltpu.CompilerParams(dimension_semantics=("parallel", "parallel"),
                                             vmem_limit_bytes=VMEM_LIMIT),
        name="in_proj",
    )(*stream, g, w)


def _ffn_kernel(*refs, final, first, fc):
    if final:
        h, refs = refs[0][0], refs[1:]
        mix = [r[0] for r in refs[:5]]
    else:
        h, refs = _read_stream(first, refs)
        mix = [r[...] for r in refs[:5]]
    wo_ref, g_ref, wu_ref, wd_ref, gf_ref, out_ref = refs[5:]
    o = jnp.concatenate(mix, axis=1).astype(BF16)
    acc = h + jnp.dot(o, wo_ref[...], preferred_element_type=F32)
    ms = jnp.mean(acc * acc, axis=-1, keepdims=True)
    hb = (acc * lax.rsqrt(ms + NORM_EPS) * g_ref[...]).astype(BF16)
    mlp = None
    for c in range(0, D_FF, fc):
        u = jnp.dot(hb, wu_ref[:, c:c + fc], preferred_element_type=F32)
        u = jnp.square(jnp.maximum(u, 0.0)).astype(BF16)
        d = jnp.dot(u, wd_ref[c:c + fc, :], preferred_element_type=F32)
        mlp = d if mlp is None else mlp + d
    y = acc + mlp
    if final:
        ms = jnp.mean(y * y, axis=-1, keepdims=True)
        y = y * lax.rsqrt(ms + NORM_EPS) * gf_ref[...]
    out_ref[...] = y


def _out_ffn(stream, lp, outs, wo, g, wu, wd, gf, tm, final):
    first = len(stream) == 2
    B = stream[0].shape[0]
    row_spec = lambda n: pl.BlockSpec((None, tm, n), lambda b, j: (b, j, 0))
    if final:
        rows_out = lp - CHUNK
        specs = [_elem_spec(tm, a.shape[-1], lambda j: CHUNK + j * tm) for a in (stream[0],) + tuple(outs)]
    else:
        rows_out = lp
        specs = _stream_specs(first, tm) + [row_spec(o.shape[-1]) for o in outs]
    return pl.pallas_call(
        functools.partial(_ffn_kernel, final=final, first=first, fc=1024),
        grid=(B, rows_out // tm),
        in_specs=specs + [_resident((D_MODEL, D_MODEL)), _resident((1, D_MODEL)), _resident((D_MODEL, D_FF)),
                          _resident((D_FF, D_MODEL)), _resident((1, D_MODEL))],
        out_specs=row_spec(D_MODEL),
        out_shape=jax.ShapeDtypeStruct((B, rows_out, D_MODEL), F32),
        compiler_params=pltpu.CompilerParams(dimension_semantics=("parallel", "parallel"),
                                             vmem_limit_bytes=VMEM_LIMIT),
        name="out_ffn",
    )(*stream, *outs, wo, g, wu, wd, gf)


def _ret_kernel(z_ref, cos_ref, sin_ref, dm_ref, qdec_ref, kdec_t_ref, gcm_ref, o_ref, s_scr):
    hm = _head_masks()
    bdm = _block_diag_mask()
    bd_ones = bdm.astype(BF16)
    z = z_ref[...]
    q, k, v, g = z[:, 0:256], z[:, 256:512], z[:, 512:768], z[:, 768:1024]
    cos, sin = cos_ref[...], sin_ref[...]
    upper = (_iota((1, GROUP_W), 1) & (HEAD_DIM - 1)) >= HEAD_DIM // 2

    def rot(t):
        swapped = jnp.where(upper, pltpu.roll(t, HEAD_DIM // 2, 1),
                            pltpu.roll(t, GROUP_W - HEAD_DIM // 2, 1))
        return t * cos + swapped * sin

    qr = rot(q)
    kr = rot(k) * HEAD_DIM ** -0.5
    yield
    s = s_scr[...]
    o = _dot(qr * qdec_ref[...], s)
    vb = v.astype(BF16)
    sc = _dot_nt(jnp.concatenate([qr * hm[h] for h in range(N_HEADS)], axis=0), kr)
    kv = _dot(kr.T * kdec_t_ref[...], vb)
    s_scr[...] = gcm_ref[...] * s + jnp.where(bdm, kv, 0.0)
    yield
    C = CHUNK
    for h in range(N_HEADS):
        p = (sc[h * C:(h + 1) * C] * dm_ref[h]).astype(BF16)
        o = o + jnp.dot(p, vb, preferred_element_type=F32) * hm[h]
    yield
    msq = _head_sum(o * o, bd_ones) * (1.0 / HEAD_DIM)
    o_ref[...] = o * lax.rsqrt(msq + NORM_EPS) * _silu(g)


def _retention(z, cos, sin, dm, qdec, kdec_t, gcm):
    B, Lp, _ = z.shape
    C = CHUNK
    rows = _rows_per_step(B, RET_ROWS)
    pos_spec = pl.BlockSpec((C, GROUP_W), lambda b, j: (j, 0))
    return pl.pallas_call(
        _per_row(_ret_kernel, rows, 1, 7, 1),
        grid=(B // rows, Lp // C),
        in_specs=[_row_spec(rows, 1024), pos_spec, pos_spec,
                  _const_spec(dm), _const_spec(qdec), _const_spec(kdec_t), _const_spec(gcm)],
        out_specs=_row_spec(rows, GROUP_W),
        out_shape=jax.ShapeDtypeStruct((B, Lp, GROUP_W), F32),
        scratch_shapes=[pltpu.VMEM((rows, GROUP_W, GROUP_W), F32)],
        compiler_params=pltpu.CompilerParams(dimension_semantics=("parallel", "arbitrary")),
        name="retention",
    )(z, cos, sin, dm, qdec, kdec_t, gcm)


def _gla_kernel(z_ref, lb_ref, ng_ref, o_ref, s_scr):
    C = CHUNK
    hm = _head_masks()
    bdm = _block_diag_mask()
    bd_ones = bdm.astype(BF16)
    z = z_ref[...]
    q, f, v, g =z[:, 0:256], z[:, 256:512], z[:, 512:768], z[:, 768:1024]
    lb = lb_ref[...]
    forget = lb + (1.0 - lb) * jax.nn.sigmoid(f)
    k = 1.0 - forget
    logf = jnp.log(forget)
    q = _silu(q) * HEAD_DIM ** -0.5
    row = _iota((C, 1), 0)
    ti, si = _iota((C, C), 0), _iota((C, C), 1)
    b = _dot_f32((ti >= si).astype(F32), logf)
    yield

    terms, shifted_v = [q * k], [v]
    for d in range(1, 8):
        valid = (row & 7) >= d
        diff = jnp.where(valid, b - pltpu.roll(b, d, 0), 0.0)
        terms.append(jnp.where(valid, q * pltpu.roll(k, d, 0) * jnp.exp(diff), 0.0))
        shifted_v.append(pltpu.roll(v, d, 0))
    band = _head_sum(jnp.concatenate(terms, axis=0), bd_ones)
    yield
    o = band[0:C] * shifted_v[0]
    for d in range(1, 8):
        o = o + band[d * C:(d + 1) * C] * shifted_v[d]

    p = jnp.zeros((N_HEADS * C, C), F32)
    m = 8
    while 2 * m <= C:
        b3 = b.reshape(C // (2 * m), 2 * m, GROUP_W)
        ref = b3[:, m - 1:m, :]
        dq = jnp.minimum(b3 - ref, 0.0).reshape(C, GROUP_W)
        dk = jnp.minimum(ref - b3, 0.0).reshape(C, GROUP_W)
        second = (row & (2 * m - 1)) >= m
        qm = jnp.where(second, q * jnp.exp(dq), 0.0)
        km = jnp.where(second, 0.0, k * jnp.exp(dk))
        lq = jnp.concatenate([qm * hm[h] for h in range(N_HEADS)], axis=0)
        sc = _dot(lq, km.T)
        same = (ti // (2 * m)) == (si // (2 * m))
        p = p + jnp.where(jnp.concatenate([same] * N_HEADS, axis=0), sc, 0.0)
        m *= 2
        yield
    vb = v.astype(BF16)
    for h in range(N_HEADS):
        o = o + jnp.dot(p[h * C:(h + 1) * C].astype(BF16), vb, preferred_element_type=F32) * hm[h]
    yield

    s = s_scr[...]
    o = o + _dot(q * jnp.exp(b), s)
    b_t = b.T
    b_last = b_t[:, C - 1:C]
    k_end = k.T * jnp.exp(b_last - b_t)
    s_scr[...] = jnp.exp(b_last) * s + jnp.where(bdm, _dot(k_end, vb), 0.0)
    yield
    msq = _head_sum(o * o, bd_ones) * (1.0 / HEAD_DIM)
    o_ref[...] = o * lax.rsqrt(msq + NORM_EPS) * ng_ref[...] * _silu(g)


def _hgrn2(z, lb, ng):
    B, Lp, _ = z.shape
    C = CHUNK
    rows = _rows_per_step(B, GLA_ROWS)
    return pl.pallas_call(
        _per_row(_gla_kernel, rows, 1, 3, 1),
        grid=(B // rows, Lp // C),
        in_specs=[_row_spec(rows, 1024), _const_spec(lb), _const_spec(ng)],
        out_specs=_row_spec(rows, GROUP_W),
        out_shape=jax.ShapeDtypeStruct((B, Lp, GROUP_W), F32),
        scratch_shapes=[pltpu.VMEM((rows, GROUP_W, GROUP_W), F32)],
        compiler_params=pltpu.CompilerParams(dimension_semantics=("parallel", "arbitrary")),
        name="hgrn2",
    )(z, lb, ng)


def _rwkv_kernel(*refs, has_vmix):
    C = CHUNK
    if has_vmix:
        (z_ref, vf_ref, mu_ref, w0_ref, wup_ref, a0_ref, aup_ref, gup_ref, kk_ref, ka_ref, rk_ref,
         lnw_ref, lnb_ref, v0_ref, vdn_ref, vup_ref, o_ref, t_scr, carry_scr) = refs
    else:
        (z_ref, mu_ref, w0_ref, wup_ref, a0_ref, aup_ref, gup_ref, kk_ref, ka_ref, rk_ref,
         lnw_ref, lnb_ref, o_ref, vf_out_ref, t_scr, carry_scr) = refs

    hm = _head_masks()
    bdm = _block_diag_mask()
    bd_ones = bdm.astype(BF16)
    row = _iota((C, 1), 0)
    ti, si = _iota((C, C), 0), _iota((C, C), 1)

    z = z_ref[...]
    z_prev = jnp.where(row == 0, carry_scr[0:1, :], pltpu.roll(z, 1, 0))
    carry_scr[0:1, :] = z[C - 1:C, :]
    zf = z + (z_prev - z) * mu_ref[...]
    r, k, v = zf[:, 0:256], zf[:, 256:512], zf[:, 512:768]
    wa, gl = zf[:, 768:896], zf[:, 896:RWKV_COLS_PAD]

    w_log = -jax.nn.softplus(-(w0_ref[...] + _dot(jnp.tanh(wa), wup_ref[...]))) - 0.5
    logw = -jnp.exp(w_log)
    a = jax.nn.sigmoid(a0_ref[...] + _dot(wa, aup_ref[...]))
    gate = _dot(jax.nn.sigmoid(gl), gup_ref[...])
    if has_vmix:
        mix = jax.nn.sigmoid(v0_ref[...] + _dot(_dot(v, vdn_ref[...]), vup_ref[...]))
        v = v + (vf_ref[...] - v) * mix
    else:
        vf_out_ref[...] = v
    kk = k * kk_ref[...]
    kk = kk / jnp.maximum(jnp.sqrt(_head_sum(kk * kk, bd_ones)), 1e-12)
    k = k * (1.0 + (a - 1.0) * ka_ref[...])
    alpha = -kk * a
    yield

    gc = _dot_f32((ti >= si).astype(F32), logw)
    yield
    gp = gc - logw
    gm = gc[C // 2:C // 2 + 1, :]
    r_mid, b_mid = r * jnp.exp(gc - gm), kk * jnp.exp(gp - gm)
    r_abs, b_abs = r * jnp.exp(gc), kk * jnp.exp(gp)
    g_t, a_t, k_t = gc.T, alpha.T, k.T
    from_mid = jnp.exp(g_t[:, C // 2:C // 2 + 1] - g_t)
    g_last = g_t[:, C - 1:C]
    to_end = jnp.exp(g_last - g_t)

    yield
    head_rows = [(_iota((GROUP_W, 1), 0) >> 6) == h for h in range(N_HEADS)]
    a_mid, k_mid = a_t * from_mid, k_t * from_mid
    rhs_t = jnp.concatenate([jnp.where(m, a_mid, 0.0) for m in head_rows]
                            + [jnp.where(m, k_mid, 0.0) for m in head_rows], axis=1)
    sc = _dot(jnp.concatenate([b_mid, r_mid], axis=0), rhs_t)
    yield

    t0 = t_scr[...]
    t0b = t0.astype(BF16)
    strict2 = jnp.concatenate([ti > si] * 2, axis=1)
    incl2 = jnp.concatenate([ti >= si] * 2, axis=1)
    first_half = _iota((1, 2 * C), 1) < C
    pairs = range(N_HEADS // 2)

    def stack(x, p):
        return jnp.concatenate([x * hm[2 * p], x * hm[2 * p + 1]], axis=0).astype(BF16)

    def block_diag(xab):
        return jnp.concatenate([jnp.where(first_half, xab, 0.0), jnp.where(first_half, 0.0, xab)],
                               axis=0).astype(BF16)

    v_stack = [stack(v, p) for p in pairs]
    rhs = jnp.dot(b_abs.astype(BF16), t0b, preferred_element_type=F32)
    for p in pairs:
        bm = jnp.where(strict2, sc[0:C, (N_HEADS + 2 * p) * C:(N_HEADS + 2 * p + 2) * C], 0.0)
        rhs = rhs + jnp.dot(bm.astype(BF16), v_stack[p], preferred_element_type=F32)
    yield

    a_ab = [jnp.where(strict2, sc[0:C, 2 * p * C:(2 * p + 2) * C], 0.0) for p in pairs]
    lower_left = lambda m: jnp.concatenate(
        [((ti & (2 * m - 1)) >= m) & ((si & (2 * m - 1)) < m) & ((ti ^ si) < 2 * m)] * 2, axis=1)
    n_ab = [jnp.where(lower_left(1), a_ab[p], 0.0) for p in pairs]
    eye2 = jnp.concatenate([ti == si] * 2, axis=1)
    for lvl in range(1, C.bit_length() - 1):
        m = 1 << lvl
        sel = lower_left(m)
        for p in pairs:
            x = jnp.where(eye2, 1.0, n_ab[p])
            y = jnp.dot(jnp.where(sel, a_ab[p], 0.0).astype(BF16), block_diag(x), preferred_element_type=F32)
            n_ab[p] = n_ab[p] + jnp.dot(x.astype(BF16), block_diag(y), preferred_element_type=F32)
        yield

    u = rhs + sum(jnp.dot(n_ab[p].astype(BF16), stack(rhs, p), preferred_element_type=F32) for p in pairs)
    yield
    o = jnp.dot(r_abs.astype(BF16), t0b, preferred_element_type=F32)
    for p in pairs:
        ao = jnp.where(incl2, sc[C:2 * C, 2 * p * C:(2 * p + 2) * C], 0.0)
        bo = jnp.where(incl2, sc[C:2 * C, (N_HEADS + 2 * p) * C:(N_HEADS + 2 * p + 2) * C], 0.0)
        o = o + jnp.dot(jnp.concatenate([ao, bo], axis=1).astype(BF16),
                        jnp.concatenate([stack(u, p), v_stack[p]], axis=0), preferred_element_type=F32)
    uv = jnp.concatenate([u, v], axis=0).astype(BF16)
    upd = _dot(jnp.concatenate([a_t * to_end, k_t * to_end], axis=1), uv)
    t_scr[...] = jnp.exp(g_last) * t0 + jnp.where(bdm, upd, 0.0)
    yield

    inv_d = 1.0 / HEAD_DIM
    dev = o - _head_sum(o, bd_ones) * inv_d
    var = _head_sum(dev * dev, bd_ones) * inv_d
    on = dev * lax.rsqrt(var + RWKV_LN_EPS) * lnw_ref[...] + lnb_ref[...]
    on = on + _head_sum(r * k * rk_ref[...], bd_ones) * v
    o_ref[...] = on * gate


def _rwkv7(z, v_first, params, vmix):
    B, Lp, _ = z.shape
    C = CHUNK
    has_vmix = vmix is not None
    rows = _rows_per_step(B, RWKV_ROWS)
    tok = lambda n: _row_spec(rows, n)
    args, specs = [z], [tok(RWKV_COLS_PAD)]
    if has_vmix:
        args.append(v_first)
        specs.append(tok(GROUP_W))
    n_row_in = len(args)
    extra = list(params) + (list(vmix) if has_vmix else [])
    args += extra
    specs += [_const_spec(a) for a in extra]
    o_shape = jax.ShapeDtypeStruct((B, Lp, GROUP_W), F32)
    n_out = 1 if has_vmix else 2
    res = pl.pallas_call(
        _per_row(functools.partial(_rwkv_kernel, has_vmix=has_vmix), rows, n_row_in, len(args), n_out),
        grid=(B // rows, Lp // C),
        in_specs=specs,
        out_specs=tok(GROUP_W) if has_vmix else [tok(GROUP_W), tok(GROUP_W)],
        out_shape=o_shape if has_vmix else [o_shape, o_shape],
        scratch_shapes=[pltpu.VMEM((rows, GROUP_W, GROUP_W), F32), pltpu.VMEM((rows, 8, RWKV_COLS_PAD), F32)],
        compiler_params=pltpu.CompilerParams(dimension_semantics=("parallel", "arbitrary")),
        name="rwkv7",
    )(*args)
    return (res, v_first) if has_vmix else (res[0], res[1])


def _s5_kernel(ua_ref, ub_ref, bst_ref, tm_ref, pm_ref, pw_re_ref, pw_im_ref, d_ref, gw_ref, gb_ref,
               oa_ref, ob_ref, c_re_scr, c_im_scr, *, nb):
    sb = S5_BLOCK
    us = [jnp.concatenate([ua_ref[pl.ds(i, nb, stride=sb), :], ub_ref[pl.ds(i, nb, stride=sb), :]], axis=1)
          for i in range(sb)]
    ucat = jnp.concatenate(us, axis=1).astype(BF16)
    w = jnp.dot(ucat, bst_ref[...], preferred_element_type=F32)
    yield
    re, im = w[:, 0:S5_LANES], w[:, S5_LANES:]
    row = _iota((nb, 1), 0)
    first = row == 0
    cr, ci = c_re_scr[0:1, :], c_im_scr[0:1, :]
    ar, ai = pw_re_ref[0:1, :], pw_im_ref[0:1, :]
    re = re + jnp.where(first, ar * cr - ai * ci, 0.0)
    im = im + jnp.where(first, ar * ci + ai * cr, 0.0)
    lvl = 0
    while (1 << lvl) < nb:
        sh = 1 << lvl
        ar, ai = pw_re_ref[lvl:lvl + 1, :], pw_im_ref[lvl:lvl + 1, :]
        sre = jnp.where(row >= sh, pltpu.roll(re, sh, 0), 0.0)
        sim = jnp.where(row >= sh, pltpu.roll(im, sh, 0), 0.0)
        re, im = re + ar * sre - ai * sim, im + ar * sim + ai * sre
        lvl += 1
    x_in = jnp.concatenate([jnp.where(first, cr, pltpu.roll(re, 1, 0)),
                            jnp.where(first, ci, pltpu.roll(im, 1, 0))], axis=1).astype(BF16)
    c_re_scr[0:1, :] = re[nb - 1:nb, :]
    c_im_scr[0:1, :] = im[nb - 1:nb, :]
    yield
    for i in range(sb):
        lo, hi = i * GROUP_W, (i + 1) * GROUP_W
        y = (jnp.dot(ucat[:, 0:hi], tm_ref[0:hi, lo:hi], preferred_element_type=F32)
             + jnp.dot(x_in, pm_ref[:, lo:hi], preferred_element_type=F32))
        y = jax.nn.gelu(y + d_ref[...] * us[i])
        out = y * jax.nn.sigmoid(_dot(y, gw_ref[...]) + gb_ref[...])
        oa_ref[pl.ds(i, nb, stride=sb), :] = out[:, 0:128]
        ob_ref[pl.ds(i, nb, stride=sb), :] = out[:, 128:256]
        if i % 2 == 1:
            yield


def _s5(za, zb, bst, tm, pm, pw_re, pw_im, d, gw, gb):
    B, Lp, _ = za.shape
    rows = _rows_per_step(B, S5_ROWS)
    nb = _pick_tile(Lp // S5_BLOCK, (176, 24, 16, 8))
    tk = nb * S5_BLOCK
    big = lambda a: pl.BlockSpec(a.shape, lambda b, j: (0,) * a.ndim, pipeline_mode=pl.Buffered(1))
    tok = pl.BlockSpec((rows, tk, 128), lambda b, j: (b, j, 0))
    small = (pw_re, pw_im, d, gw, gb)
    half = jax.ShapeDtypeStruct((B, Lp, 128), F32)
    return pl.pallas_call(
        _per_row(functools.partial(_s5_kernel, nb=nb), rows, 2, 10, 2),
        grid=(B // rows, Lp // tk),
        in_specs=[tok, tok, big(bst), big(tm), big(pm)] + [_const_spec(a) for a in small],
        out_specs=[tok, tok],
        out_shape=[half, half],
        scratch_shapes=[pltpu.VMEM((rows, 8, S5_LANES), F32), pltpu.VMEM((rows, 8, S5_LANES), F32)],
        compiler_params=pltpu.CompilerParams(dimension_semantics=("parallel", "arbitrary"),
                                             vmem_limit_bytes=VMEM_LIMIT),
        name="s5",
    )(za, zb, bst, tm, pm, *small)


def _rope_tables(lp):
    half = HEAD_DIM // 2
    pos = jnp.arange(lp) - PAD_FRONT
    inv_freq = ROPE_BASE ** (-jnp.arange(half, dtype=F32) / half)
    ang = pos.astype(F32)[:, None] * inv_freq[None, :]
    cos, sin = jnp.cos(ang), jnp.sin(ang)
    cos_t = jnp.tile(jnp.concatenate([cos, cos], axis=1), (1, N_HEADS))
    sin_t = jnp.tile(jnp.concatenate([-sin, sin], axis=1), (1, N_HEADS))
    return cos_t, sin_t


def _retention_tables():
    C = CHUNK
    log_g = jnp.log1p(-jnp.exp2(-5.0 - jnp.arange(N_HEADS, dtype=F32)))
    t = jnp.arange(C, dtype=F32)
    rel = t[:, None] - t[None, :]
    causal = rel >= 0
    dm = jnp.where(causal[None], jnp.exp(jnp.where(causal, rel, 0.0)[None] * log_g[:, None, None]), 0.0)
    lane_g = jnp.repeat(log_g, HEAD_DIM)
    qdec = jnp.exp((t + 1.0)[:, None] * lane_g[None, :])
    kdec_t = jnp.exp(lane_g[:, None] * (C - 1.0 - t)[None, :])
    gcm = jnp.broadcast_to(jnp.exp(C * lane_g)[:, None], (GROUP_W, GROUP_W))
    return dm, qdec, kdec_t, gcm


def _s5_tables(a_re, a_im, log_dt, b_re, b_im, c_re, c_im):
    G, sb = S5_GROUPS, S5_BLOCK
    hi = lax.Precision.HIGHEST
    lam_re, lam_im = a_re.astype(F32), a_im.astype(F32)
    dt = jnp.exp(log_dt.astype(F32))[:, None]
    mag, ph = jnp.exp(lam_re * dt), lam_im * dt
    ab_re, ab_im = mag * jnp.cos(ph), mag * jnp.sin(ph)
    den = lam_re * lam_re + lam_im * lam_im
    nr, ni = ab_re - 1.0, ab_im
    zc_re = (nr * lam_re + ni * lam_im) / den
    zc_im = (ni * lam_re - nr * lam_im) / den
    b_re, b_im = b_re.astype(F32), b_im.astype(F32)
    bb_re = zc_re[..., None] * b_re - zc_im[..., None] * b_im
    bb_im = zc_re[..., None] * b_im + zc_im[..., None] * b_re
    c_re, c_im = c_re.astype(F32), c_im.astype(F32)
    eye = jnp.eye(G, dtype=F32)

    def power(n):
        m, p = jnp.exp(lam_re * dt * n), lam_im * dt * n
        return m * jnp.cos(p), m * jnp.sin(p)

    to_state = lambda m: jnp.einsum('gpc,gh->gchp', m, eye, precision=hi).reshape(GROUP_W, S5_LANES)
    from_state = lambda m: jnp.einsum('gcp,gh->hpgc', m, eye, precision=hi).reshape(S5_LANES, GROUP_W)
    chan = lambda m: jnp.einsum('gcd,gh->gdhc', m, eye, precision=hi).reshape(GROUP_W, GROUP_W)

    bst = []
    for i in range(sb):
        pr, pi = power(float(sb - 1 - i))
        wr = pr[..., None] * bb_re - pi[..., None] * bb_im
        wi = pr[..., None] * bb_im + pi[..., None] * bb_re
        bst.append(jnp.concatenate([to_state(wr), to_state(wi)], axis=1))
    bst = jnp.concatenate(bst, axis=0)

    pm, kern = [], []
    for n in range(sb + 1):
        pr, pi = power(float(n))
        m_re = c_re * pr[:, None, :] - c_im * pi[:, None, :]
        m_im = c_re * pi[:, None, :] + c_im * pr[:, None, :]
        if n >= 1:
            pm.append(jnp.concatenate([from_state(m_re), from_state(-m_im)], axis=0))
        if n < sb:
            kern.append(jnp.einsum('gcp,gpd->gcd', m_re, bb_re, precision=hi)
                        - jnp.einsum('gcp,gpd->gcd', m_im, bb_im, precision=hi))
    pm = jnp.concatenate(pm, axis=1)
    zero = jnp.zeros((GROUP_W, GROUP_W), F32)
    tm = jnp.concatenate([jnp.concatenate([chan(kern[i - j]) if i >= j else zero for i in range(sb)], axis=1)
                          for j in range(sb)], axis=0)

    lv = jnp.arange(16, dtype=F32)[:, None, None]
    m, p = jnp.exp(lam_re * dt * sb * 2.0 ** lv), lam_im * dt * sb * 2.0 ** lv
    pw_re, pw_im = (m * jnp.cos(p)).reshape(16, S5_LANES), (m * jnp.sin(p)).reshape(16, S5_LANES)
    return bst.astype(BF16), tm.astype(BF16), pm.astype(BF16), pw_re, pw_im


def _pick_tile(lp, candidates):
    for t in candidates:
        if lp % t == 0:
            return t
    return CHUNK


def kernel(x, meta_tokens, norm_mix_g, w_in, hgrn_lb_logits, hgrn_norm_g, rwkv_mu, rwkv_w0, rwkv_w_up,
           rwkv_a0, rwkv_a_up, rwkv_g_up, rwkv_k_k, rwkv_k_a, rwkv_r_k, rwkv_ln_w, rwkv_ln_b, rwkv_v0,
           rwkv_v_down, rwkv_v_up, s5_a_re, s5_a_im, s5_log_dt, s5_b_re, s5_b_im, s5_c_re, s5_c_im, s5_d,
           s5_glu_w, s5_glu_b, w_out, norm_ffn_g, w_ffn_up, w_ffn_down, norm_f_g):
    B, seq, _ = x.shape
    depth = w_in.shape[0]
    assert seq % CHUNK == 0
    lp = CHUNK + seq
    tm = _pick_tile(lp, (384, 256, 128))
    tm_last = _pick_tile(seq, (512, 256, 128))
    row2 = lambda a: a.astype(F32).reshape(1, -1)

    meta = jnp.broadcast_to(meta_tokens.astype(x.dtype)[None], (B, N_META, D_MODEL))
    head = jnp.concatenate([jnp.zeros((B, PAD_FRONT, D_MODEL), x.dtype), meta], axis=1)
    if depth > 1:
        stream = (jnp.concatenate([head, x[:, :tm - CHUNK]], axis=1), x)
    else:
        stream = (jnp.concatenate([head, x], axis=1),)

    p = jax.nn.softmax(hgrn_lb_logits.astype(F32), axis=0)
    lower_bounds = jnp.cumsum(p, axis=0) - p[0]
    cos_t, sin_t = _rope_tables(lp)
    dm, qdec, kdec_t, gcm = _retention_tables()

    def pad_rows(a, n):
        return jnp.pad(a, ((0, n - a.shape[0]), (0, 0)))

    v_first = None
    for l in range(depth):
        w = w_in[l]
        w_pad = jnp.concatenate([w[:, :2048 + RWKV_COLS], jnp.zeros((D_MODEL, RWKV_COLS_PAD - RWKV_COLS), w.dtype),
                                 w[:, 2048 + RWKV_COLS:]], axis=1).astype(BF16)
        z_ret, z_hgrn, z_rwkv, z_s5a, z_s5b = _in_proj(stream, lp, row2(norm_mix_g[l]), w_pad, tm)

        o_ret = _retention(z_ret, cos_t, sin_t, dm, qdec, kdec_t, gcm)
        o_hgrn = _hgrn2(z_hgrn, row2(lower_bounds[l]), row2(jnp.tile(hgrn_norm_g[l], N_HEADS)))

        mu = jnp.pad(rwkv_mu[l].astype(F32), (0, RWKV_COLS_PAD - RWKV_COLS)).reshape(1, -1)
        w_up = pad_rows(rwkv_w_up[l], 128).astype(BF16)
        a_up = jnp.concatenate([jnp.zeros_like(rwkv_a_up[l]), rwkv_a_up[l]], axis=0).astype(BF16)
        g_up = pad_rows(rwkv_g_up[l], GROUP_W).astype(BF16)
        params = (mu, row2(rwkv_w0[l]), w_up, row2(rwkv_a0[l]), a_up, g_up, row2(rwkv_k_k[l]),
                  row2(rwkv_k_a[l]), row2(rwkv_r_k[l]), row2(rwkv_ln_w[l]), row2(rwkv_ln_b[l]))
        vmix = None
        if l > 0:
            v_down = jnp.pad(rwkv_v_down[l - 1], ((0, 0), (0, 128 - rwkv_v_down.shape[-1]))).astype(BF16)
            v_up = pad_rows(rwkv_v_up[l - 1], 128).astype(BF16)
            vmix = (row2(rwkv_v0[l - 1]), v_down, v_up)
        o_rwkv, v_first = _rwkv7(z_rwkv, v_first, params, vmix)

        tabs = _s5_tables(s5_a_re[l], s5_a_im[l], s5_log_dt[l], s5_b_re[l], s5_b_im[l], s5_c_re[l], s5_c_im[l])
        o_s5a, o_s5b = _s5(z_s5a, z_s5b, *tabs, row2(s5_d[l]), s5_glu_w[l].astype(BF16), row2(s5_glu_b[l]))

        last = l == depth - 1
        h = _out_ffn(stream, lp, (o_ret, o_hgrn, o_rwkv, o_s5a, o_s5b), w_out[l].astype(BF16),
                     row2(norm_ffn_g[l]), w_ffn_up[l].astype(BF16), w_ffn_down[l].astype(BF16),
                     row2(norm_f_g), tm_last if last else tm, final=last)
        stream = (h,)
    return h
```

```python
import functools

import jax
import jax.numpy as jnp
from jax import lax
from jax.experimental import pallas as pl
from jax.experimental.pallas import tpu as pltpu

F32 = jnp.float32
BF16 = jnp.bfloat16

D_MODEL = 1024
N_META = 16
HEAD_DIM = 64
GROUP_W = 256
N_HEADS = 4
D_FF = 4096
ROPE_BASE = 10000.0
NORM_EPS = 1e-6
RWKV_LN_EPS = 64e-5
S5_GROUPS = 16
S5_CH = 16
S5_STATE = 64
S5_LANES = S5_GROUPS * S5_STATE
RWKV_COLS = 1056
RWKV_COLS_PAD = 1152
N_Z = 1024 + 1024 + RWKV_COLS_PAD + 256

CHUNK = 128
PAD_FRONT = CHUNK - N_META
VMEM_LIMIT = 56 * 1024 * 1024
RET_ROWS, GLA_ROWS, RWKV_ROWS, S5_ROWS = 8, 4, 8, 2
S5_BLOCK = 8


def _rows_per_step(batch, rows):
    while batch % rows:
        rows -= 1
    return rows


def _dot(a, b):
    return jnp.dot(a.astype(BF16), b.astype(BF16), preferred_element_type=F32)


def _dot_f32(a, b):
    return jnp.dot(a, b, preferred_element_type=F32, precision=lax.Precision.HIGHEST)


def _dot_nt(a, b):
    return lax.dot_general(a.astype(BF16), b.astype(BF16), (((1,), (1,)), ((), ())),
                           preferred_element_type=F32)


def _split_dot(x, w_bf16):
    hi = x.astype(BF16)
    lo = (x - hi.astype(F32)).astype(BF16)
    return (jnp.dot(hi, w_bf16, preferred_element_type=F32)
            + jnp.dot(lo, w_bf16, preferred_element_type=F32))


def _iota(shape, dim):
    return lax.broadcasted_iota(jnp.int32, shape, dim)


def _head_masks():
    lane = _iota((1, GROUP_W), 1)
    return [((lane >> 6) == h).astype(F32) for h in range(N_HEADS)]


def _block_diag_mask():
    return (_iota((GROUP_W, GROUP_W), 0) >> 6) == (_iota((GROUP_W, GROUP_W), 1) >> 6)


def _head_sum(x, bd_ones):
    return _split_dot(x, bd_ones)


def _silu(x):
    return x * jax.nn.sigmoid(x)


def _per_row(body, rows, n_row_in, n_in, n_out):
    def kern(*refs):
        ins, outs, scrs = refs[:n_in], refs[n_in:n_in + n_out], refs[n_in + n_out:]

        @pl.when(pl.program_id(1) == 0)
        def _():
            for s in scrs:
                s[...] = jnp.zeros_like(s)

        live = [body(*[r.at[bi] for r in ins[:n_row_in]], *ins[n_row_in:], *[r.at[bi] for r in outs],
                     *[s.at[bi] for s in scrs]) for bi in range(rows)]
        while live:
            for g in list(live):
                if next(g, "done") == "done":
                    live.remove(g)
    return kern


def _row_spec(rows, n):
    return pl.BlockSpec((rows, CHUNK, n), lambda b, j: (b, j, 0))


def _const_spec(a):
    return pl.BlockSpec(a.shape, lambda b, j: (0,) * a.ndim)


def _elem_spec(tm, n, offset):
    return pl.BlockSpec((pl.Element(1), pl.Element(tm), pl.Element(n)),
                        lambda b, j: (b, pl.multiple_of(offset(j), 128), 0))


def _stream_specs(first, tm):
    if not first:
        return [pl.BlockSpec((None, tm, D_MODEL), lambda b, j: (b, j, 0))]
    return [pl.BlockSpec((None, tm, D_MODEL), lambda b, j: (b, 0, 0)),
            _elem_spec(tm, D_MODEL, lambda j: jnp.maximum(j * tm - CHUNK, 0))]


def _read_stream(first, refs):
    if not first:
        return refs[0][...], refs[1:]
    return jnp.where(pl.program_id(1) == 0, refs[0][...], refs[1][0]), refs[2:]


def _resident(shape):
    return pl.BlockSpec(shape, lambda b, j: (0,) * len(shape), pipeline_mode=pl.Buffered(1))


def _in_proj_kernel(*refs, tm, first):
    x, (g_ref, w_ref, zr_ref, zh_ref, zw_ref, zsa_ref, zsb_ref) = _read_stream(first, refs)
    ms = jnp.mean(x * x, axis=-1, keepdims=True)
    xn = x * lax.rsqrt(ms + NORM_EPS) * g_ref[...]
    row = pl.program_id(1) * tm + _iota((tm, 1), 0)
    xb = jnp.where(row < PAD_FRONT, 0.0, xn).astype(BF16)
    zr_ref[...] = jnp.dot(xb, w_ref[:, 0:1024], preferred_element_type=F32)
    zh_ref[...] = jnp.dot(xb, w_ref[:, 1024:2048], preferred_element_type=F32)
    zw_ref[...] = jnp.dot(xb, w_ref[:, 2048:2048 + RWKV_COLS_PAD], preferred_element_type=F32)
    zs = jnp.dot(xb, w_ref[:, 2048 + RWKV_COLS_PAD:N_Z], preferred_element_type=F32)
    zsa_ref[...] = zs[:, 0:128]
    zsb_ref[...] = zs[:, 128:256]


def _in_proj(stream, lp, g, w, tm):
    first = len(stream) == 2
    B = stream[0].shape[0]
    row_spec = lambda n: pl.BlockSpec((None, tm, n), lambda b, j: (b, j, 0))
    widths = (1024, 1024, RWKV_COLS_PAD, 128, 128)
    return pl.pallas_call(
        functools.partial(_in_proj_kernel, tm=tm, first=first),
        grid=(B, lp // tm),
        in_specs=_stream_specs(first, tm) + [_resident((1, D_MODEL)), _resident((D_MODEL, N_Z))],
        out_specs=[row_spec(n) for n in widths],
        out_shape=[jax.ShapeDtypeStruct((B, lp, n), F32) for n in widths],
        compiler_params=pltpu.CompilerParams(dimension_semantics=("parallel", "parallel"),
                                             vmem_limit_bytes=VMEM_LIMIT),
        name="in_proj",
    )(*stream, g, w)


def _ffn_kernel(*refs, final, first, fc):
    if final:
        h, refs = refs[0][0], refs[1:]
        mix = [r[0] for r in refs[:5]]
    else:
        h, refs = _read_stream(first, refs)
        mix = [r[...] for r in refs[:5]]
    wo_ref, g_ref, wu_ref, wd_ref, gf_ref, out_ref = refs[5:]
    o = jnp.concatenate(mix, axis=1).astype(BF16)
    acc = h + jnp.dot(o, wo_ref[...], preferred_element_type=F32)
    ms = jnp.mean(acc * acc, axis=-1, keepdims=True)
    hb = (acc * lax.rsqrt(ms + NORM_EPS) * g_ref[...]).astype(BF16)
    mlp = None
    for c in range(0, D_FF, fc):
        u = jnp.dot(hb, wu_ref[:, c:c + fc], preferred_element_type=F32)
        u = jnp.square(jnp.maximum(u, 0.0)).astype(BF16)
        d = jnp.dot(u, wd_ref[c:c + fc, :], preferred_element_type=F32)
        mlp = d if mlp is None else mlp + d
    y = acc + mlp
    if final:
        ms = jnp.mean(y * y, axis=-1, keepdims=True)
        y = y * lax.rsqrt(ms + NORM_EPS) * gf_ref[...]
    out_ref[...] = y


def _out_ffn(stream, lp, outs, wo, g, wu, wd, gf, tm, final):
    first = len(stream) == 2
    B = stream[0].shape[0]
    row_spec = lambda n: pl.BlockSpec((None, tm, n), lambda b, j: (b, j, 0))
    if final:
        rows_out = lp - CHUNK
        specs = [_elem_spec(tm, a.shape[-1], lambda j: CHUNK + j * tm) for a in (stream[0],) + tuple(outs)]
    else:
        rows_out = lp
        specs = _stream_specs(first, tm) + [row_spec(o.shape[-1]) for o in outs]
    return pl.pallas_call(
        functools.partial(_ffn_kernel, final=final, first=first, fc=1024),
        grid=(B, rows_out // tm),
        in_specs=specs + [_resident((D_MODEL, D_MODEL)), _resident((1, D_MODEL)), _resident((D_MODEL, D_FF)),
                          _resident((D_FF, D_MODEL)), _resident((1, D_MODEL))],
        out_specs=row_spec(D_MODEL),
        out_shape=jax.ShapeDtypeStruct((B, rows_out, D_MODEL), F32),
        compiler_params=pltpu.CompilerParams(dimension_semantics=("parallel", "parallel"),
                                             vmem_limit_bytes=VMEM_LIMIT),
        name="out_ffn",
    )(*stream, *outs, wo, g, wu, wd, gf)


def _ret_kernel(z_ref, cos_ref, sin_ref, dm_ref, qdec_ref, kdec_t_ref, gcm_ref, o_ref, s_scr):
    hm = _head_masks()
    bdm = _block_diag_mask()
    bd_ones = bdm.astype(BF16)
    z = z_ref[...]
    q, k, v, g = z[:, 0:256], z[:, 256:512], z[:, 512:768], z[:, 768:1024]
    cos, sin = cos_ref[...], sin_ref[...]
    upper = (_iota((1, GROUP_W), 1) & (HEAD_DIM - 1)) >= HEAD_DIM // 2

    def rot(t):
        swapped = jnp.where(upper, pltpu.roll(t, HEAD_DIM // 2, 1),
                            pltpu.roll(t, GROUP_W - HEAD_DIM // 2, 1))
        return t * cos + swapped * sin

    qr = rot(q)
    kr = rot(k) * HEAD_DIM ** -0.5
    yield
    s = s_scr[...]
    o = _dot(qr * qdec_ref[...], s)
    vb = v.astype(BF16)
    sc = _dot_nt(jnp.concatenate([qr * hm[h] for h in range(N_HEADS)], axis=0), kr)
    kv = _dot(kr.T * kdec_t_ref[...], vb)
    s_scr[...] = gcm_ref[...] * s + jnp.where(bdm, kv, 0.0)
    yield
    C = CHUNK
    for h in range(N_HEADS):
        p = (sc[h * C:(h + 1) * C] * dm_ref[h]).astype(BF16)
        o = o + jnp.dot(p, vb, preferred_element_type=F32) * hm[h]
    yield
    msq = _head_sum(o * o, bd_ones) * (1.0 / HEAD_DIM)
    o_ref[...] = o * lax.rsqrt(msq + NORM_EPS) * _silu(g)


def _retention(z, cos, sin, dm, qdec, kdec_t, gcm):
    B, Lp, _ = z.shape
    C = CHUNK
    rows = _rows_per_step(B, RET_ROWS)
    pos_spec = pl.BlockSpec((C, GROUP_W), lambda b, j: (j, 0))
    return pl.pallas_call(
        _per_row(_ret_kernel, rows, 1, 7, 1),
        grid=(B // rows, Lp // C),
        in_specs=[_row_spec(rows, 1024), pos_spec, pos_spec,
                  _const_spec(dm), _const_spec(qdec), _const_spec(kdec_t), _const_spec(gcm)],
        out_specs=_row_spec(rows, GROUP_W),
        out_shape=jax.ShapeDtypeStruct((B, Lp, GROUP_W), F32),
        scratch_shapes=[pltpu.VMEM((rows, GROUP_W, GROUP_W), F32)],
        compiler_params=pltpu.CompilerParams(dimension_semantics=("parallel", "arbitrary")),
        name="retention",
    )(z, cos, sin, dm, qdec, kdec_t, gcm)


def _gla_kernel(z_ref, lb_ref, ng_ref, o_ref, s_scr):
    C = CHUNK
    hm = _head_masks()
    bdm = _block_diag_mask()
    bd_ones = bdm.astype(BF16)
    z = z_ref[...]
    q, f, v, g =z[:, 0:256], z[:, 256:512], z[:, 512:768], z[:, 768:1024]
    lb = lb_ref[...]
    forget = lb + (1.0 - lb) * jax.nn.sigmoid(f)
    k = 1.0 - forget
    logf = jnp.log(forget)
    q = _silu(q) * HEAD_DIM ** -0.5
    row = _iota((C, 1), 0)
    ti, si = _iota((C, C), 0), _iota((C, C), 1)
    b = _dot_f32((ti >= si).astype(F32), logf)
    yield

    o = jnp.dot((q * k).astype(BF16), bd_ones, preferred_element_type=F32) * v

    head_lane = [(_iota((1, GROUP_W), 1) >> 6) == h for h in range(N_HEADS)]
    b8 = b.reshape(C // 8, 8, GROUP_W)
    p = None
    m = 1
    while 2 * m <= C:
        if m == 1:
            ref = jnp.where((row & 1) == 1, pltpu.roll(b, 1, 0), b)
        elif m == 2:
            ref = jnp.where(_iota((1, 8, 1), 1) < 4, b8[:, 1:2, :], b8[:, 5:6, :]).reshape(C, GROUP_W)
        else:
            b3 = b.reshape(C // (2 * m), 2 * m, GROUP_W)
            ref = jnp.broadcast_to(b3[:, m - 1:m, :], b3.shape).reshape(C, GROUP_W)
        second = (row & (2 * m - 1)) >= m
        qm = jnp.where(second, q * jnp.exp(b - ref), 0.0).astype(BF16)
        km = jnp.where(second, 0.0, k * jnp.exp(ref - b))
        lq = jnp.concatenate([jnp.where(hl, qm, 0.0) for hl in head_lane], axis=0)
        sc = jnp.dot(lq, km.T.astype(BF16), preferred_element_type=F32)
        if 2 * m < C:
            sc = jnp.where(jnp.concatenate([(ti ^ si) < 2 * m] * N_HEADS, axis=0), sc, 0.0)
        p = sc if p is None else p + sc
        m *= 2
        yield
    vb = v.astype(BF16)
    for h in range(N_HEADS):
        o = o + jnp.dot(p[h * C:(h + 1) * C].astype(BF16), vb, preferred_element_type=F32) * hm[h]
    yield

    s = s_scr[...]
    o = o + _dot(q * jnp.exp(b), s)
    b_t = b.T
    b_last = b_t[:, C - 1:C]
    k_end = k.T * jnp.exp(b_last - b_t)
    s_scr[...] = jnp.exp(b_last) * s + jnp.where(bdm, _dot(k_end, vb), 0.0)
    yield
    msq = _head_sum(o * o, bd_ones) * (1.0 / HEAD_DIM)
    o_ref[...] = o * lax.rsqrt(msq + NORM_EPS) * ng_ref[...] * _silu(g)


def _hgrn2(z, lb, ng):
    B, Lp, _ = z.shape
    C = CHUNK
    rows = _rows_per_step(B, GLA_ROWS)
    return pl.pallas_call(
        _per_row(_gla_kernel, rows, 1, 3, 1),
        grid=(B // rows, Lp // C),
        in_specs=[_row_spec(rows, 1024), _const_spec(lb), _const_spec(ng)],
        out_specs=_row_spec(rows, GROUP_W),
        out_shape=jax.ShapeDtypeStruct((B, Lp, GROUP_W), F32),
        scratch_shapes=[pltpu.VMEM((rows, GROUP_W, GROUP_W), F32)],
        compiler_params=pltpu.CompilerParams(dimension_semantics=("parallel", "arbitrary")),
        name="hgrn2",
    )(z, lb, ng)


def _rwkv_kernel(*refs, has_vmix):
    C = CHUNK
    if has_vmix:
        (z_ref, vf_ref, mu_ref, w0_ref, wup_ref, a0_ref, aup_ref, gup_ref, kk_ref, ka_ref, rk_ref,
         lnw_ref, lnb_ref, v0_ref, vdn_ref, vup_ref, o_ref, t_scr, carry_scr) = refs
    else:
        (z_ref, mu_ref, w0_ref, wup_ref, a0_ref, aup_ref, gup_ref, kk_ref, ka_ref, rk_ref,
         lnw_ref, lnb_ref, o_ref, vf_out_ref, t_scr, carry_scr) = refs

    hm = _head_masks()
    bdm = _block_diag_mask()
    bd_ones = bdm.astype(BF16)
    row = _iota((C, 1), 0)
    ti, si = _iota((C, C), 0), _iota((C, C), 1)

    z = z_ref[...]
    z_prev = jnp.where(row == 0, carry_scr[0:1, :], pltpu.roll(z, 1, 0))
    carry_scr[0:1, :] = z[C - 1:C, :]
    zf = z + (z_prev - z) * mu_ref[...]
    r, k, v = zf[:, 0:256], zf[:, 256:512], zf[:, 512:768]
    wa, gl = zf[:, 768:896], zf[:, 896:RWKV_COLS_PAD]

    w_log = -jax.nn.softplus(-(w0_ref[...] + _dot(jnp.tanh(wa), wup_ref[...]))) - 0.5
    logw = -jnp.exp(w_log)
    a = jax.nn.sigmoid(a0_ref[...] + _dot(wa, aup_ref[...]))
    gate = _dot(jax.nn.sigmoid(gl), gup_ref[...])
    if has_vmix:
        mix = jax.nn.sigmoid(v0_ref[...] + _dot(_dot(v, vdn_ref[...]), vup_ref[...]))
        v = v + (vf_ref[...] - v) * mix
    else:
        vf_out_ref[...] = v
    kk = k * kk_ref[...]
    kk = kk / jnp.maximum(jnp.sqrt(_head_sum(kk * kk, bd_ones)), 1e-12)
    k = k * (1.0 + (a - 1.0) * ka_ref[...])
    alpha = -kk * a
    yield

    gc = _dot_f32((ti >= si).astype(F32), logw)
    yield
    gp = gc - logw
    gm = gc[C // 2:C // 2 + 1, :]
    r_mid, b_mid = r * jnp.exp(gc - gm), kk * jnp.exp(gp - gm)
    r_abs, b_abs = r * jnp.exp(gc), kk * jnp.exp(gp)
    g_t, a_t, k_t = gc.T, alpha.T, k.T
    from_mid = jnp.exp(g_t[:, C // 2:C // 2 + 1] - g_t)
    g_last = g_t[:, C - 1:C]
    to_end = jnp.exp(g_last - g_t)

    yield
    head_rows = [(_iota((GROUP_W, 1), 0) >> 6) == h for h in range(N_HEADS)]
    a_mid, k_mid = a_t * from_mid, k_t * from_mid
    rhs_t = jnp.concatenate([jnp.where(m, a_mid, 0.0) for m in head_rows]
                            + [jnp.where(m, k_mid, 0.0) for m in head_rows], axis=1)
    sc = _dot(jnp.concatenate([b_mid, r_mid], axis=0), rhs_t)
    yield

    t0 = t_scr[...]
    t0b = t0.astype(BF16)
    strict2 = jnp.concatenate([ti > si] * 2, axis=1)
    incl2 = jnp.concatenate([ti >= si] * 2, axis=1)
    first_half = _iota((1, 2 * C), 1) < C
    pairs = range(N_HEADS // 2)

    def stack(x, p):
        return jnp.concatenate([x * hm[2 * p], x * hm[2 * p + 1]], axis=0).astype(BF16)

    def block_diag(xab):
        return jnp.concatenate([jnp.where(first_half, xab, 0.0), jnp.where(first_half, 0.0, xab)],
                               axis=0).astype(BF16)

    v_stack = [stack(v, p) for p in pairs]
    rhs = jnp.dot(b_abs.astype(BF16), t0b, preferred_element_type=F32)
    for p in pairs:
        bm = jnp.where(strict2, sc[0:C, (N_HEADS + 2 * p) * C:(N_HEADS + 2 * p + 2) * C], 0.0)
        rhs = rhs + jnp.dot(bm.astype(BF16), v_stack[p], preferred_element_type=F32)
    yield

    a_ab = [jnp.where(strict2, sc[0:C, 2 * p * C:(2 * p + 2) * C], 0.0) for p in pairs]
    lower_left = lambda m: jnp.concatenate(
        [((ti & (2 * m - 1)) >= m) & ((si & (2 * m - 1)) < m) & ((ti ^ si) < 2 * m)] * 2, axis=1)
    n_ab = [jnp.where(lower_left(1), a_ab[p], 0.0) for p in pairs]
    eye2 = jnp.concatenate([ti == si] * 2, axis=1)
    for lvl in range(1, C.bit_length() - 1):
        m = 1 << lvl
        sel = lower_left(m)
        for p in pairs:
            x = jnp.where(eye2, 1.0, n_ab[p])
            y = jnp.dot(jnp.where(sel, a_ab[p], 0.0).astype(BF16), block_diag(x), preferred_element_type=F32)
            n_ab[p] = n_ab[p] + jnp.dot(x.astype(BF16), block_diag(y), preferred_element_type=F32)
        yield

    u = rhs + sum(jnp.dot(n_ab[p].astype(BF16), stack(rhs, p), preferred_element_type=F32) for p in pairs)
    yield
    o = jnp.dot(r_abs.astype(BF16), t0b, preferred_element_type=F32)
    for p in pairs:
        ao = jnp.where(incl2, sc[C:2 * C, 2 * p * C:(2 * p + 2) * C], 0.0)
        bo = jnp.where(incl2, sc[C:2 * C, (N_HEADS + 2 * p) * C:(N_HEADS + 2 * p + 2) * C], 0.0)
        o = o + jnp.dot(jnp.concatenate([ao, bo], axis=1).astype(BF16),
                        jnp.concatenate([stack(u, p), v_stack[p]], axis=0), preferred_element_type=F32)
    uv = jnp.concatenate([u, v], axis=0).astype(BF16)
    upd = _dot(jnp.concatenate([a_t * to_end, k_t * to_end], axis=1), uv)
    t_scr[...] = jnp.exp(g_last) * t0 + jnp.where(bdm, upd, 0.0)
    yield

    inv_d = 1.0 / HEAD_DIM
    dev = o - _head_sum(o, bd_ones) * inv_d
    var = _head_sum(dev * dev, bd_ones) * inv_d
    on = dev * lax.rsqrt(var + RWKV_LN_EPS) * lnw_ref[...] + lnb_ref[...]
    on = on + _head_sum(r * k * rk_ref[...], bd_ones) * v
    o_ref[...] = on * gate


def _rwkv7(z, v_first, params, vmix):
    B, Lp, _ = z.shape
    C = CHUNK
    has_vmix = vmix is not None
    rows = _rows_per_step(B, RWKV_ROWS)
    tok = lambda n: _row_spec(rows, n)
    args, specs = [z], [tok(RWKV_COLS_PAD)]
    if has_vmix:
        args.append(v_first)
        specs.append(tok(GROUP_W))
    n_row_in = len(args)
    extra = list(params) + (list(vmix) if has_vmix else [])
    args += extra
    specs += [_const_spec(a) for a in extra]
    o_shape = jax.ShapeDtypeStruct((B, Lp, GROUP_W), F32)
    n_out = 1 if has_vmix else 2
    res = pl.pallas_call(
        _per_row(functools.partial(_rwkv_kernel, has_vmix=has_vmix), rows, n_row_in, len(args), n_out),
        grid=(B // rows, Lp // C),
        in_specs=specs,
        out_specs=tok(GROUP_W) if has_vmix else [tok(GROUP_W), tok(GROUP_W)],
        out_shape=o_shape if has_vmix else [o_shape, o_shape],
        scratch_shapes=[pltpu.VMEM((rows, GROUP_W, GROUP_W), F32), pltpu.VMEM((rows, 8, RWKV_COLS_PAD), F32)],
        compiler_params=pltpu.CompilerParams(dimension_semantics=("parallel", "arbitrary")),
        name="rwkv7",
    )(*args)
    return (res, v_first) if has_vmix else (res[0], res[1])


def _s5_kernel(ua_ref, ub_ref, bst_ref, tm_ref, pm_ref, pw_re_ref, pw_im_ref, d_ref, gw_ref, gb_ref,
               oa_ref, ob_ref, c_re_scr, c_im_scr, *, nb):
    sb = S5_BLOCK
    us = [jnp.concatenate([ua_ref[pl.ds(i, nb, stride=sb), :], ub_ref[pl.ds(i, nb, stride=sb), :]], axis=1)
          for i in range(sb)]
    ucat = jnp.concatenate(us, axis=1).astype(BF16)
    w = jnp.dot(ucat, bst_ref[...], preferred_element_type=F32)
    yield
    re, im = w[:, 0:S5_LANES], w[:, S5_LANES:]
    row = _iota((nb, 1), 0)
    first = row == 0
    cr, ci = c_re_scr[0:1, :], c_im_scr[0:1, :]
    ar, ai = pw_re_ref[0:1, :], pw_im_ref[0:1, :]
    re = re + jnp.where(first, ar * cr - ai * ci, 0.0)
    im = im + jnp.where(first, ar * ci + ai * cr, 0.0)
    lvl = 0
    while (1 << lvl) < nb:
        sh = 1 << lvl
        ar, ai = pw_re_ref[lvl:lvl + 1, :], pw_im_ref[lvl:lvl + 1, :]
        sre = jnp.where(row >= sh, pltpu.roll(re, sh, 0), 0.0)
        sim = jnp.where(row >= sh, pltpu.roll(im, sh, 0), 0.0)
        re, im = re + ar * sre - ai * sim, im + ar * sim + ai * sre
        lvl += 1
    x_in = jnp.concatenate([jnp.where(first, cr, pltpu.roll(re, 1, 0)),
                            jnp.where(first, ci, pltpu.roll(im, 1, 0))], axis=1).astype(BF16)
    c_re_scr[0:1, :] = re[nb - 1:nb, :]
    c_im_scr[0:1, :] = im[nb - 1:nb, :]
    yield
    for i in range(sb):
        lo, hi = i * GROUP_W, (i + 1) * GROUP_W
        y = (jnp.dot(ucat[:, 0:hi], tm_ref[0:hi, lo:hi], preferred_element_type=F32)
             + jnp.dot(x_in, pm_ref[:, lo:hi], preferred_element_type=F32))
        y = jax.nn.gelu(y + d_ref[...] * us[i])
        out = y * jax.nn.sigmoid(_dot(y, gw_ref[...]) + gb_ref[...])
        oa_ref[pl.ds(i, nb, stride=sb), :] = out[:, 0:128]
        ob_ref[pl.ds(i, nb, stride=sb), :] = out[:, 128:256]
        if i % 2 == 1:
            yield


def _s5(za, zb, bst, tm, pm, pw_re, pw_im, d, gw, gb):
    B, Lp, _ = za.shape
    rows = _rows_per_step(B, S5_ROWS)
    nb = _pick_tile(Lp // S5_BLOCK, (176, 24, 16, 8))
    tk = nb * S5_BLOCK
    big = lambda a: pl.BlockSpec(a.shape, lambda b, j: (0,) * a.ndim, pipeline_mode=pl.Buffered(1))
    tok = pl.BlockSpec((rows, tk, 128), lambda b, j: (b, j, 0))
    small = (pw_re, pw_im, d, gw, gb)
    half = jax.ShapeDtypeStruct((B, Lp, 128), F32)
    return pl.pallas_call(
        _per_row(functools.partial(_s5_kernel, nb=nb), rows, 2, 10, 2),
        grid=(B // rows, Lp // tk),
        in_specs=[tok, tok, big(bst), big(tm), big(pm)] + [_const_spec(a) for a in small],
        out_specs=[tok, tok],
        out_shape=[half, half],
        scratch_shapes=[pltpu.VMEM((rows, 8, S5_LANES), F32), pltpu.VMEM((rows, 8, S5_LANES), F32)],
        compiler_params=pltpu.CompilerParams(dimension_semantics=("parallel", "arbitrary"),
                                             vmem_limit_bytes=VMEM_LIMIT),
        name="s5",
    )(za, zb, bst, tm, pm, *small)


def _rope_tables(lp):
    half = HEAD_DIM // 2
    pos = jnp.arange(lp) - PAD_FRONT
    inv_freq = ROPE_BASE ** (-jnp.arange(half, dtype=F32) / half)
    ang = pos.astype(F32)[:, None] * inv_freq[None, :]
    cos, sin = jnp.cos(ang), jnp.sin(ang)
    cos_t = jnp.tile(jnp.concatenate([cos, cos], axis=1), (1, N_HEADS))
    sin_t = jnp.tile(jnp.concatenate([-sin, sin], axis=1), (1, N_HEADS))
    return cos_t, sin_t


def _retention_tables():
    C = CHUNK
    log_g = jnp.log1p(-jnp.exp2(-5.0 - jnp.arange(N_HEADS, dtype=F32)))
    t = jnp.arange(C, dtype=F32)
    rel = t[:, None] - t[None, :]
    causal = rel >= 0
    dm = jnp.where(causal[None], jnp.exp(jnp.where(causal, rel, 0.0)[None] * log_g[:, None, None]), 0.0)
    lane_g = jnp.repeat(log_g, HEAD_DIM)
    qdec = jnp.exp((t + 1.0)[:, None] * lane_g[None, :])
    kdec_t = jnp.exp(lane_g[:, None] * (C - 1.0 - t)[None, :])
    gcm = jnp.broadcast_to(jnp.exp(C * lane_g)[:, None], (GROUP_W, GROUP_W))
    return dm, qdec, kdec_t, gcm


def _s5_tables(a_re, a_im, log_dt, b_re, b_im, c_re, c_im):
    G, sb = S5_GROUPS, S5_BLOCK
    hi = lax.Precision.HIGHEST
    lam_re, lam_im = a_re.astype(F32), a_im.astype(F32)
    dt = jnp.exp(log_dt.astype(F32))[:, None]
    mag, ph = jnp.exp(lam_re * dt), lam_im * dt
    ab_re, ab_im = mag * jnp.cos(ph), mag * jnp.sin(ph)
    den = lam_re * lam_re + lam_im * lam_im
    nr, ni = ab_re - 1.0, ab_im
    zc_re = (nr * lam_re + ni * lam_im) / den
    zc_im = (ni * lam_re - nr * lam_im) / den
    b_re, b_im = b_re.astype(F32), b_im.astype(F32)
    bb_re = zc_re[..., None] * b_re - zc_im[..., None] * b_im
    bb_im = zc_re[..., None] * b_im + zc_im[..., None] * b_re
    c_re, c_im = c_re.astype(F32), c_im.astype(F32)
    eye = jnp.eye(G, dtype=F32)
    n = jnp.arange(sb + 1, dtype=F32)[:, None, None]
    mag_n, ph_n = jnp.exp(lam_re * dt * n), lam_im * dt * n
    pr, pi = mag_n * jnp.cos(ph_n), mag_n * jnp.sin(ph_n)

    qr, qi = pr[sb - 1::-1, :, :, None], pi[sb - 1::-1, :, :, None]
    w = jnp.stack([qr * bb_re - qi * bb_im, qr * bb_im + qi * bb_re], axis=0)
    bst = (jnp.transpose(w, (1, 2, 4, 0, 3))[:, :, :, :, None, :]
           * eye[None, :, None, None, :, None]).reshape(sb * GROUP_W, 2 * S5_LANES)

    m_re = c_re * pr[:, :, None, :] - c_im * pi[:, :, None, :]
    m_im = c_re * pi[:, :, None, :] + c_im * pr[:, :, None, :]
    coef = jnp.stack([m_re[1:], -m_im[1:]], axis=0)
    pm = (jnp.transpose(coef, (0, 4, 1, 2, 3))[:, None]
          * eye[None, :, None, None, :, None]).reshape(2 * S5_LANES, sb * GROUP_W)

    kern = (jnp.einsum('ngcp,gpd->ngcd', m_re[:sb], bb_re, precision=hi)
            - jnp.einsum('ngcp,gpd->ngcd', m_im[:sb], bb_im, precision=hi))
    lag = jnp.arange(sb)[None, :] - jnp.arange(sb)[:, None]
    k_ji = jnp.where((lag >= 0)[:, :, None, None, None], kern[jnp.clip(lag, 0, sb - 1)], 0.0)
    tm = (jnp.transpose(k_ji, (0, 2, 4, 1, 3))[:, :, :, :, None, :]
          * eye[None, :, None, None, :, None]).reshape(sb * GROUP_W, sb * GROUP_W)

    lv = jnp.arange(16, dtype=F32)[:, None, None]
    m, p = jnp.exp(lam_re * dt * sb * 2.0 ** lv), lam_im * dt * sb * 2.0 ** lv
    pw_re, pw_im = (m * jnp.cos(p)).reshape(16, S5_LANES), (m * jnp.sin(p)).reshape(16, S5_LANES)
    return bst.astype(BF16), tm.astype(BF16), pm.astype(BF16), pw_re, pw_im


def _pick_tile(lp, candidates):
    for t in candidates:
        if lp % t == 0:
            return t
    return CHUNK


def kernel(x, meta_tokens, norm_mix_g, w_in, hgrn_lb_logits, hgrn_norm_g, rwkv_mu, rwkv_w0, rwkv_w_up,
           rwkv_a0, rwkv_a_up, rwkv_g_up, rwkv_k_k, rwkv_k_a, rwkv_r_k, rwkv_ln_w, rwkv_ln_b, rwkv_v0,
           rwkv_v_down, rwkv_v_up, s5_a_re, s5_a_im, s5_log_dt, s5_b_re, s5_b_im, s5_c_re, s5_c_im, s5_d,
           s5_glu_w, s5_glu_b, w_out, norm_ffn_g, w_ffn_up, w_ffn_down, norm_f_g):
    B, seq, _ = x.shape
    depth = w_in.shape[0]
    assert seq % CHUNK == 0
    lp = CHUNK + seq
    tm = _pick_tile(lp, (704, 384, 256, 128))
    tm_last = _pick_tile(seq, (512, 256, 128))
    row2 = lambda a: a.astype(F32).reshape(1, -1)

    meta = jnp.broadcast_to(meta_tokens.astype(x.dtype)[None], (B, N_META, D_MODEL))
    head = jnp.concatenate([jnp.zeros((B, PAD_FRONT, D_MODEL), x.dtype), meta], axis=1)
    if depth > 1:
        stream = (jnp.concatenate([head, x[:, :tm - CHUNK]], axis=1), x)
    else:
        stream = (jnp.concatenate([head, x], axis=1),)

    p = jax.nn.softmax(hgrn_lb_logits.astype(F32), axis=0)
    lower_bounds = jnp.cumsum(p, axis=0) - p[0]
    cos_t, sin_t = _rope_tables(lp)
    dm, qdec, kdec_t, gcm = _retention_tables()

    def pad_rows(a, n):
        return jnp.pad(a, ((0, n - a.shape[0]), (0, 0)))

    v_first = None
    for l in range(depth):
        w = w_in[l]
        w_pad = jnp.concatenate([w[:, :2048 + RWKV_COLS], jnp.zeros((D_MODEL, RWKV_COLS_PAD - RWKV_COLS), w.dtype),
                                 w[:, 2048 + RWKV_COLS:]], axis=1).astype(BF16)
        z_ret, z_hgrn, z_rwkv, z_s5a, z_s5b = _in_proj(stream, lp, row2(norm_mix_g[l]), w_pad, tm)

        o_ret = _retention(z_ret, cos_t, sin_t, dm, qdec, kdec_t, gcm)
        o_hgrn = _hgrn2(z_hgrn, row2(lower_bounds[l]), row2(jnp.tile(hgrn_norm_g[l], N_HEADS)))

        mu = jnp.pad(rwkv_mu[l].astype(F32), (0, RWKV_COLS_PAD - RWKV_COLS)).reshape(1, -1)
        w_up = pad_rows(rwkv_w_up[l], 128).astype(BF16)
        a_up = jnp.concatenate([jnp.zeros_like(rwkv_a_up[l]), rwkv_a_up[l]], axis=0).astype(BF16)
        g_up = pad_rows(rwkv_g_up[l], GROUP_W).astype(BF16)
        params = (mu, row2(rwkv_w0[l]), w_up, row2(rwkv_a0[l]), a_up, g_up, row2(rwkv_k_k[l]),
                  row2(rwkv_k_a[l]), row2(rwkv_r_k[l]), row2(rwkv_ln_w[l]), row2(rwkv_ln_b[l]))
        vmix = None
        if l > 0:
            v_down = jnp.pad(rwkv_v_down[l - 1], ((0, 0), (0, 128 - rwkv_v_down.shape[-1]))).astype(BF16)
            v_up = pad_rows(rwkv_v_up[l - 1], 128).astype(BF16)
            vmix = (row2(rwkv_v0[l - 1]), v_down, v_up)
        o_rwkv, v_first = _rwkv7(z_rwkv, v_first, params, vmix)

        tabs = _s5_tables(s5_a_re[l], s5_a_im[l], s5_log_dt[l], s5_b_re[l], s5_b_im[l], s5_c_re[l], s5_c_im[l])
        o_s5a, o_s5b = _s5(z_s5a, z_s5b, *tabs, row2(s5_d[l]), s5_glu_w[l].astype(BF16), row2(s5_glu_b[l]))

        last = l == depth - 1
        h = _out_ffn(stream, lp, (o_ret, o_hgrn, o_rwkv, o_s5a, o_s5b), w_out[l].astype(BF16),
                     row2(norm_ffn_g[l]), w_ffn_up[l].astype(BF16), w_ffn_down[l].astype(BF16),
                     row2(norm_f_g), tm_last if last else tm, final=last)
        stream = (h,)
    return h
```

```python
import functools

import jax
import jax.numpy as jnp
from jax import lax
from jax.experimental import pallas as pl
from jax.experimental.pallas import tpu as pltpu

F32 = jnp.float32
BF16 = jnp.bfloat16

D_MODEL = 1024
N_META = 16
HEAD_DIM = 64
GROUP_W = 256
N_HEADS = 4
D_FF = 4096
ROPE_BASE = 10000.0
NORM_EPS = 1e-6
RWKV_LN_EPS = 64e-5
S5_GROUPS = 16
S5_CH = 16
S5_STATE = 64
S5_LANES = S5_GROUPS * S5_STATE
RWKV_COLS = 1056
RWKV_COLS_PAD = 1152
N_Z = 1024 + 1024 + RWKV_COLS_PAD + 256

CHUNK = 128
PAD_FRONT = CHUNK - N_META
VMEM_LIMIT = 56 * 1024 * 1024
RET_ROWS, GLA_ROWS, RWKV_ROWS, S5_ROWS = 8, 4, 8, 2
S5_BLOCK = 8


def _rows_per_step(batch, rows):
    while batch % rows:
        rows -= 1
    return rows


def _dot(a, b):
    return jnp.dot(a.astype(BF16), b.astype(BF16), preferred_element_type=F32)


def _dot_f32(a, b):
    return jnp.dot(a, b, preferred_element_type=F32, precision=lax.Precision.HIGHEST)


def _dot_nt(a, b):
    return lax.dot_general(a.astype(BF16), b.astype(BF16), (((1,), (1,)), ((), ())),
                           preferred_element_type=F32)


def _split_dot(x, w_bf16):
    hi = x.astype(BF16)
    lo = (x - hi.astype(F32)).astype(BF16)
    return (jnp.dot(hi, w_bf16, preferred_element_type=F32)
            + jnp.dot(lo, w_bf16, preferred_element_type=F32))


def _iota(shape, dim):
    return lax.broadcasted_iota(jnp.int32, shape, dim)


def _head_masks():
    lane = _iota((1, GROUP_W), 1)
    return [((lane >> 6) == h).astype(F32) for h in range(N_HEADS)]


def _block_diag_mask():
    return (_iota((GROUP_W, GROUP_W), 0) >> 6) == (_iota((GROUP_W, GROUP_W), 1) >> 6)


def _head_sum(x, bd_ones):
    return _split_dot(x, bd_ones)


def _silu(x):
    return x * jax.nn.sigmoid(x)


def _interleaved(parts, rows):
    def kern(*refs):
        n_in, n_out = sum(p[2] for p in parts), sum(p[3] for p in parts)
        ins, outs, scrs = refs[:n_in], refs[n_in:n_in + n_out], refs[n_in + n_out:]

        @pl.when(pl.program_id(1) == 0)
        def _():
            for s in scrs:
                s[...] = jnp.zeros_like(s)

        live = []
        for bi in range(rows):
            i0 = o0 = s0 = 0
            for body, n_row_in, n_i, n_o, n_s in parts:
                p_in, p_out, p_scr = ins[i0:i0 + n_i], outs[o0:o0 + n_o], scrs[s0:s0 + n_s]
                i0, o0, s0 = i0 + n_i, o0 + n_o, s0 + n_s
                live.append(body(*[r.at[bi] for r in p_in[:n_row_in]], *p_in[n_row_in:],
                                 *[r.at[bi] for r in p_out], *[s.at[bi] for s in p_scr]))
        while live:
            for g in list(live):
                if next(g, "done") == "done":
                    live.remove(g)
    return kern


def _const_spec(a):
    return pl.BlockSpec(a.shape, lambda b, j: (0,) * a.ndim)


def _elem_spec(tm, n, offset):
    return pl.BlockSpec((pl.Element(1), pl.Element(tm), pl.Element(n)),
                        lambda b, j: (b, pl.multiple_of(offset(j), 128), 0))


def _stream_specs(first, tm):
    if not first:
        return [pl.BlockSpec((None, tm, D_MODEL), lambda b, j: (b, j, 0))]
    return [pl.BlockSpec((None, CHUNK, D_MODEL), lambda b, j: (b, 0, 0)),
            _elem_spec(tm, D_MODEL, lambda j: jnp.maximum(j * tm - CHUNK, 0))]


def _read_stream(first, refs):
    if not first:
        return refs[0][...], refs[1:]
    x = refs[1][0]
    tile0 = jnp.concatenate([refs[0][...], x[:x.shape[0] - CHUNK]], axis=0)
    return jnp.where(pl.program_id(1) == 0, tile0, x), refs[2:]


def _resident(shape):
    return pl.BlockSpec(shape, lambda b, j: (0,) * len(shape), pipeline_mode=pl.Buffered(1))


def _in_proj_kernel(*refs, tm, first):
    x, (g_ref, w_ref, zr_ref, zh_ref, zw_ref, zsa_ref, zsb_ref) = _read_stream(first, refs)
    ms = jnp.mean(x * x, axis=-1, keepdims=True)
    xn = x * lax.rsqrt(ms + NORM_EPS) * g_ref[...]
    row = pl.program_id(1) * tm + _iota((tm, 1), 0)
    xb = jnp.where(row < PAD_FRONT, 0.0, xn).astype(BF16)
    zr_ref[...] = jnp.dot(xb, w_ref[:, 0:1024], preferred_element_type=F32)
    zh_ref[...] = jnp.dot(xb, w_ref[:, 1024:2048], preferred_element_type=F32)
    zw_ref[...] = jnp.dot(xb, w_ref[:, 2048:2048 + RWKV_COLS_PAD], preferred_element_type=F32)
    zs = jnp.dot(xb, w_ref[:, 2048 + RWKV_COLS_PAD:N_Z], preferred_element_type=F32)
    zsa_ref[...] = zs[:, 0:128]
    zsb_ref[...] = zs[:, 128:256]


def _in_proj(stream, lp, g, w, tm):
    first = len(stream) == 2
    B = stream[0].shape[0]
    row_spec = lambda n: pl.BlockSpec((None, tm, n), lambda b, j: (b, j, 0))
    widths = (1024, 1024, RWKV_COLS_PAD, 128, 128)
    return pl.pallas_call(
        functools.partial(_in_proj_kernel, tm=tm, first=first),
        grid=(B, lp // tm),
        in_specs=_stream_specs(first, tm) + [_resident((1, D_MODEL)), _resident((D_MODEL, N_Z))],
        out_specs=[row_spec(n) for n in widths],
        out_shape=[jax.ShapeDtypeStruct((B, lp, n), F32) for n in widths],
        compiler_params=pltpu.CompilerParams(dimension_semantics=("parallel", "parallel"),
                                             vmem_limit_bytes=VMEM_LIMIT),
        name="in_proj",
    )(*stream, g, w)


def _ffn_kernel(*refs, final, first, fc):
    if final:
        h, refs = refs[0][0], refs[1:]
        mix = [r[0] for r in refs[:5]]
    else:
        h, refs = _read_stream(first, refs)
        mix = [r[...] for r in refs[:5]]
    wo_ref, g_ref, wu_ref, wd_ref, gf_ref, out_ref = refs[5:]
    o = jnp.concatenate(mix, axis=1).astype(BF16)
    acc = h + jnp.dot(o, wo_ref[...], preferred_element_type=F32)
    ms = jnp.mean(acc * acc, axis=-1, keepdims=True)
    hb = (acc * lax.rsqrt(ms + NORM_EPS) * g_ref[...]).astype(BF16)
    mlp = None
    for c in range(0, D_FF, fc):
        u = jnp.dot(hb, wu_ref[:, c:c + fc], preferred_element_type=F32)
        u = jnp.square(jnp.maximum(u, 0.0)).astype(BF16)
        d = jnp.dot(u, wd_ref[c:c + fc, :], preferred_element_type=F32)
        mlp = d if mlp is None else mlp + d
    y = acc + mlp
    if final:
        ms = jnp.mean(y * y, axis=-1, keepdims=True)
        y = y * lax.rsqrt(ms + NORM_EPS) * gf_ref[...]
    out_ref[...] = y


def _out_ffn(stream, lp, outs, wo, g, wu, wd, gf, tm, final):
    first = len(stream) == 2
    B = stream[0].shape[0]
    row_spec = lambda n: pl.BlockSpec((None, tm, n), lambda b, j: (b, j, 0))
    if final:
        rows_out = lp - CHUNK
        specs = [_elem_spec(tm, a.shape[-1], lambda j: CHUNK + j * tm) for a in (stream[0],) + tuple(outs)]
    else:
        rows_out = lp
        specs = _stream_specs(first, tm) + [row_spec(o.shape[-1]) for o in outs]
    return pl.pallas_call(
        functools.partial(_ffn_kernel, final=final, first=first, fc=1024),
        grid=(B, rows_out // tm),
        in_specs=specs + [_resident((D_MODEL, D_MODEL)), _resident((1, D_MODEL)), _resident((D_MODEL, D_FF)),
                          _resident((D_FF, D_MODEL)), _resident((1, D_MODEL))],
        out_specs=row_spec(D_MODEL),
        out_shape=jax.ShapeDtypeStruct((B, rows_out, D_MODEL), F32),
        compiler_params=pltpu.CompilerParams(dimension_semantics=("parallel", "parallel"),
                                             vmem_limit_bytes=VMEM_LIMIT),
        name="out_ffn",
    )(*stream, *outs, wo, g, wu, wd, gf)


def _ret_kernel(z_ref, cos_ref, sin_ref, dm_ref, qdec_ref, kdec_t_ref, gcm_ref, o_ref, s_scr):
    hm = _head_masks()
    bdm = _block_diag_mask()
    bd_ones = bdm.astype(BF16)
    z = z_ref[...]
    q, k, v, g = z[:, 0:256], z[:, 256:512], z[:, 512:768], z[:, 768:1024]
    cos, sin = cos_ref[...], sin_ref[...]
    upper = (_iota((1, GROUP_W), 1) & (HEAD_DIM - 1)) >= HEAD_DIM // 2

    def rot(t):
        swapped = jnp.where(upper, pltpu.roll(t, HEAD_DIM // 2, 1),
                            pltpu.roll(t, GROUP_W - HEAD_DIM // 2, 1))
        return t * cos + swapped * sin

    qr = rot(q)
    kr = rot(k) * HEAD_DIM ** -0.5
    yield
    s = s_scr[...]
    o = _dot(qr * qdec_ref[...], s)
    vb = v.astype(BF16)
    sc = _dot_nt(jnp.concatenate([qr * hm[h] for h in range(N_HEADS)], axis=0), kr)
    kv = _dot(kr.T * kdec_t_ref[...], vb)
    s_scr[...] = gcm_ref[...] * s + jnp.where(bdm, kv, 0.0)
    yield
    C = CHUNK
    for h in range(N_HEADS):
        p = (sc[h * C:(h + 1) * C] * dm_ref[h]).astype(BF16)
        o = o + jnp.dot(p, vb, preferred_element_type=F32) * hm[h]
    yield
    msq = _head_sum(o * o, bd_ones) * (1.0 / HEAD_DIM)
    o_ref[...] = o * lax.rsqrt(msq + NORM_EPS) * _silu(g)


def _gla_kernel(z_ref, lb_ref, ng_ref, o_ref, s_scr):
    C = CHUNK
    hm = _head_masks()
    bdm = _block_diag_mask()
    bd_ones = bdm.astype(BF16)
    z = z_ref[...]
    q, f, v, g =z[:, 0:256], z[:, 256:512], z[:, 512:768], z[:, 768:1024]
    lb = lb_ref[...]
    forget = lb + (1.0 - lb) * jax.nn.sigmoid(f)
    k = 1.0 - forget
    logf = jnp.log(forget)
    q = _silu(q) * HEAD_DIM ** -0.5
    row = _iota((C, 1), 0)
    ti, si = _iota((C, C), 0), _iota((C, C), 1)
    b = _dot_f32((ti >= si).astype(F32), logf)
    yield

    o = jnp.dot((q * k).astype(BF16), bd_ones, preferred_element_type=F32) * v

    head_lane = [(_iota((1, GROUP_W), 1) >> 6) == h for h in range(N_HEADS)]
    b8 = b.reshape(C // 8, 8, GROUP_W)
    p = None
    m = 1
    while 2 * m <= C:
        if m == 1:
            ref = jnp.where((row & 1) == 1, pltpu.roll(b, 1, 0), b)
        elif m == 2:
            ref = jnp.where(_iota((1, 8, 1), 1) < 4, b8[:, 1:2, :], b8[:, 5:6, :]).reshape(C, GROUP_W)
        else:
            b3 = b.reshape(C // (2 * m), 2 * m, GROUP_W)
            ref = jnp.broadcast_to(b3[:, m - 1:m, :], b3.shape).reshape(C, GROUP_W)
        second = (row & (2 * m - 1)) >= m
        qm = jnp.where(second, q * jnp.exp(b - ref), 0.0).astype(BF16)
        km = jnp.where(second, 0.0, k * jnp.exp(ref - b))
        lq = jnp.concatenate([jnp.where(hl, qm, 0.0) for hl in head_lane], axis=0)
        sc = jnp.dot(lq, km.T.astype(BF16), preferred_element_type=F32)
        if 2 * m < C:
            sc = jnp.where(jnp.concatenate([(ti ^ si) < 2 * m] * N_HEADS, axis=0), sc, 0.0)
        p = sc if p is None else p + sc
        m *= 2
        yield
    vb = v.astype(BF16)
    for h in range(N_HEADS):
        o = o + jnp.dot(p[h * C:(h + 1) * C].astype(BF16), vb, preferred_element_type=F32) * hm[h]
    yield

    s = s_scr[...]
    o = o + _dot(q * jnp.exp(b), s)
    b_t = b.T
    b_last = b_t[:, C - 1:C]
    k_end = k.T * jnp.exp(b_last - b_t)
    s_scr[...] = jnp.exp(b_last) * s + jnp.where(bdm, _dot(k_end, vb), 0.0)
    yield
    msq = _head_sum(o * o, bd_ones) * (1.0 / HEAD_DIM)
    o_ref[...] = o * lax.rsqrt(msq + NORM_EPS) * ng_ref[...] * _silu(g)


def _rwkv_kernel(*refs, has_vmix):
    C = CHUNK
    if has_vmix:
        (z_ref, vf_ref, mu_ref, w0_ref, wup_ref, a0_ref, aup_ref, gup_ref, kk_ref, ka_ref, rk_ref,
         lnw_ref, lnb_ref, v0_ref, vdn_ref, vup_ref, o_ref, t_scr, carry_scr) = refs
    else:
        (z_ref, mu_ref, w0_ref, wup_ref, a0_ref, aup_ref, gup_ref, kk_ref, ka_ref, rk_ref,
         lnw_ref, lnb_ref, o_ref, vf_out_ref, t_scr, carry_scr) = refs

    hm = _head_masks()
    bdm = _block_diag_mask()
    bd_ones = bdm.astype(BF16)
    row = _iota((C, 1), 0)
    ti, si = _iota((C, C), 0), _iota((C, C), 1)

    z = z_ref[...]
    z_prev = jnp.where(row == 0, carry_scr[0:1, :], pltpu.roll(z, 1, 0))
    carry_scr[0:1, :] = z[C - 1:C, :]
    zf = z + (z_prev - z) * mu_ref[...]
    r, k, v = zf[:, 0:256], zf[:, 256:512], zf[:, 512:768]
    wa, gl = zf[:, 768:896], zf[:, 896:RWKV_COLS_PAD]

    w_log = -jax.nn.softplus(-(w0_ref[...] + _dot(jnp.tanh(wa), wup_ref[...]))) - 0.5
    logw = -jnp.exp(w_log)
    a = jax.nn.sigmoid(a0_ref[...] + _dot(wa, aup_ref[...]))
    gate = _dot(jax.nn.sigmoid(gl), gup_ref[...])
    if has_vmix:
        mix = jax.nn.sigmoid(v0_ref[...] + _dot(_dot(v, vdn_ref[...]), vup_ref[...]))
        v = v + (vf_ref[...] - v) * mix
    else:
        vf_out_ref[...] = v
    kk = k * kk_ref[...]
    kk = kk / jnp.maximum(jnp.sqrt(_head_sum(kk * kk, bd_ones)), 1e-12)
    k = k * (1.0 + (a - 1.0) * ka_ref[...])
    alpha = -kk * a
    yield

    gc = _dot_f32((ti >= si).astype(F32), logw)
    yield
    gp = gc - logw
    gm = gc[C // 2:C // 2 + 1, :]
    r_mid, b_mid = r * jnp.exp(gc - gm), kk * jnp.exp(gp - gm)
    r_abs, b_abs = r * jnp.exp(gc), kk * jnp.exp(gp)
    g_t, a_t, k_t = gc.T, alpha.T, k.T
    from_mid = jnp.exp(g_t[:, C // 2:C // 2 + 1] - g_t)
    g_last = g_t[:, C - 1:C]
    to_end = jnp.exp(g_last - g_t)

    yield
    head_rows = [(_iota((GROUP_W, 1), 0) >> 6) == h for h in range(N_HEADS)]
    a_mid, k_mid = a_t * from_mid, k_t * from_mid
    rhs_t = jnp.concatenate([jnp.where(m, a_mid, 0.0) for m in head_rows]
                            + [jnp.where(m, k_mid, 0.0) for m in head_rows], axis=1)
    sc = _dot(jnp.concatenate([b_mid, r_mid], axis=0), rhs_t)
    yield

    t0 = t_scr[...]
    t0b = t0.astype(BF16)
    strict2 = jnp.concatenate([ti > si] * 2, axis=1)
    incl2 = jnp.concatenate([ti >= si] * 2, axis=1)
    first_half = _iota((1, 2 * C), 1) < C
    pairs = range(N_HEADS // 2)

    def stack(x, p):
        return jnp.concatenate([x * hm[2 * p], x * hm[2 * p + 1]], axis=0).astype(BF16)

    def block_diag(xab):
        return jnp.concatenate([jnp.where(first_half, xab, 0.0), jnp.where(first_half, 0.0, xab)],
                               axis=0).astype(BF16)

    v_stack = [stack(v, p) for p in pairs]
    rhs = jnp.dot(b_abs.astype(BF16), t0b, preferred_element_type=F32)
    for p in pairs:
        bm = jnp.where(strict2, sc[0:C, (N_HEADS + 2 * p) * C:(N_HEADS + 2 * p + 2) * C], 0.0)
        rhs = rhs + jnp.dot(bm.astype(BF16), v_stack[p], preferred_element_type=F32)
    yield

    a_ab = [jnp.where(strict2, sc[0:C, 2 * p * C:(2 * p + 2) * C], 0.0) for p in pairs]
    lower_left = lambda m: jnp.concatenate(
        [((ti & (2 * m - 1)) >= m) & ((si & (2 * m - 1)) < m) & ((ti ^ si) < 2 * m)] * 2, axis=1)
    n_ab = [jnp.where(lower_left(1), a_ab[p], 0.0) for p in pairs]
    eye2 = jnp.concatenate([ti == si] * 2, axis=1)
    for lvl in range(1, C.bit_length() - 1):
        m = 1 << lvl
        sel = lower_left(m)
        for p in pairs:
            x = jnp.where(eye2, 1.0, n_ab[p])
            y = jnp.dot(jnp.where(sel, a_ab[p], 0.0).astype(BF16), block_diag(x), preferred_element_type=F32)
            n_ab[p] = n_ab[p] + jnp.dot(x.astype(BF16), block_diag(y), preferred_element_type=F32)
        yield

    u = rhs + sum(jnp.dot(n_ab[p].astype(BF16), stack(rhs, p), preferred_element_type=F32) for p in pairs)
    yield
    o = jnp.dot(r_abs.astype(BF16), t0b, preferred_element_type=F32)
    for p in pairs:
        ao = jnp.where(incl2, sc[C:2 * C, 2 * p * C:(2 * p + 2) * C], 0.0)
        bo = jnp.where(incl2, sc[C:2 * C, (N_HEADS + 2 * p) * C:(N_HEADS + 2 * p + 2) * C], 0.0)
        o = o + jnp.dot(jnp.concatenate([ao, bo], axis=1).astype(BF16),
                        jnp.concatenate([stack(u, p), v_stack[p]], axis=0), preferred_element_type=F32)
    uv = jnp.concatenate([u, v], axis=0).astype(BF16)
    upd = _dot(jnp.concatenate([a_t * to_end, k_t * to_end], axis=1), uv)
    t_scr[...] = jnp.exp(g_last) * t0 + jnp.where(bdm, upd, 0.0)
    yield

    inv_d = 1.0 / HEAD_DIM
    dev = o - _head_sum(o, bd_ones) * inv_d
    var = _head_sum(dev * dev, bd_ones) * inv_d
    on = dev * lax.rsqrt(var + RWKV_LN_EPS) * lnw_ref[...] + lnb_ref[...]
    on = on + _head_sum(r * k * rk_ref[...], bd_ones) * v
    o_ref[...] = on * gate


def _chunk_mixer(body, rows_per_step, row_args, const_specs, const_args, n_out, scratch, name):
    B, Lp, _ = row_args[0].shape
    rows = _rows_per_step(B, rows_per_step)
    tok = lambda n: pl.BlockSpec((rows, CHUNK, n), lambda b, j: (b, j, 0))
    res = pl.pallas_call(
        _interleaved([(body, len(row_args), len(row_args) + len(const_args), n_out, len(scratch))], rows),
        grid=(B // rows, Lp // CHUNK),
        in_specs=[tok(a.shape[-1]) for a in row_args] + list(const_specs),
        out_specs=[tok(GROUP_W)] * n_out,
        out_shape=[jax.ShapeDtypeStruct((B, Lp, GROUP_W), F32)] * n_out,
        scratch_shapes=[pltpu.VMEM((rows,) + s, F32) for s in scratch],
        compiler_params=pltpu.CompilerParams(dimension_semantics=("parallel", "arbitrary")),
        name=name,
    )(*row_args, *const_args)
    return res


def _retention(z, cos, sin, dm, qdec, kdec_t, gcm):
    pos_spec = pl.BlockSpec((CHUNK, GROUP_W), lambda b, j: (j, 0))
    consts = (dm, qdec, kdec_t, gcm)
    return _chunk_mixer(_ret_kernel, RET_ROWS, [z], [pos_spec, pos_spec] + [_const_spec(a) for a in consts],
                        (cos, sin) + consts, 1, [(GROUP_W, GROUP_W)], "retention")[0]


def _hgrn2(z, lb, ng):
    return _chunk_mixer(_gla_kernel, GLA_ROWS, [z], [_const_spec(lb), _const_spec(ng)], (lb, ng), 1,
                        [(GROUP_W, GROUP_W)], "hgrn2")[0]


def _rwkv7(z, v_first, params, vmix):
    has_vmix = vmix is not None
    consts = tuple(params) + (tuple(vmix) if has_vmix else ())
    res = _chunk_mixer(functools.partial(_rwkv_kernel, has_vmix=has_vmix), RWKV_ROWS,
                       [z, v_first] if has_vmix else [z], [_const_spec(a) for a in consts], consts,
                       1 if has_vmix else 2, [(GROUP_W, GROUP_W), (8, RWKV_COLS_PAD)], "rwkv7")
    return (res[0], v_first) if has_vmix else (res[0], res[1])


def _merge_pieces(sources, src_piece, dst_piece):
    piece = _iota((1, 128), 1) >> 4
    out = None
    for x, s, d in zip(sources, src_piece, dst_piece):
        x = x if s == d else pltpu.roll(x, (16 * (d - s)) % 128, 1)
        out = x if out is None else jnp.where(piece == d, x, out)
    return out


def _s5_kernel(ua_ref, ub_ref, bst_ref, tm_ref, pm_ref, pw_re_ref, pw_im_ref, d_ref, gw_ref, gb_ref,
               oa_ref, ob_ref, c_re_scr, c_im_scr, *, nb):
    sb, n_pairs = S5_BLOCK, S5_GROUPS // 2
    ua = [ua_ref[pl.ds(i, nb, stride=sb), :] for i in range(sb)]
    ub = [ub_ref[pl.ds(i, nb, stride=sb), :] for i in range(sb)]
    tokens = list(range(sb))
    ucat = []
    for q in range(n_pairs):
        src = ua if q < n_pairs // 2 else ub
        ucat.append(jnp.concatenate(
            [_merge_pieces(src, [(2 * q + gg) % 8] * sb, tokens) for gg in range(2)], axis=1).astype(BF16))
    yield
    w = [jnp.dot(ucat[q], bst_ref[q], preferred_element_type=F32) for q in range(n_pairs)]
    re = jnp.concatenate([x[:, 0:128] for x in w], axis=1)
    im = jnp.concatenate([x[:, 128:256] for x in w], axis=1)
    yield
    row = _iota((nb, 1), 0)
    first = row == 0
    cr, ci = c_re_scr[0:1, :], c_im_scr[0:1, :]
    ar, ai = pw_re_ref[0:1, :], pw_im_ref[0:1, :]
    re = re + jnp.where(first, ar * cr - ai * ci, 0.0)
    im = im + jnp.where(first, ar * ci + ai * cr, 0.0)
    lvl = 0
    while (1 << lvl) < nb:
        sh = 1 << lvl
        ar, ai = pw_re_ref[lvl:lvl + 1, :], pw_im_ref[lvl:lvl + 1, :]
        sre = jnp.where(row >= sh, pltpu.roll(re, sh, 0), 0.0)
        sim = jnp.where(row >= sh, pltpu.roll(im, sh, 0), 0.0)
        re, im = re + ar * sre - ai * sim, im + ar * sim + ai * sre
        lvl += 1
    x_re = jnp.where(first, cr, pltpu.roll(re, 1, 0)).astype(BF16)
    x_im = jnp.where(first, ci, pltpu.roll(im, 1, 0)).astype(BF16)
    c_re_scr[0:1, :] = re[nb - 1:nb, :]
    c_im_scr[0:1, :] = im[nb - 1:nb, :]
    yield
    y = []
    for q in range(n_pairs):
        x_in = jnp.concatenate([x_re[:, 128 * q:128 * (q + 1)], x_im[:, 128 * q:128 * (q + 1)]], axis=1)
        y.append(jnp.dot(ucat[q], tm_ref[q], preferred_element_type=F32)
                 + jnp.dot(x_in, pm_ref[q], preferred_element_type=F32))
    yield
    for i in range(sb):
        halves = []
        for t in range(2):
            groups = range(8 * t, 8 * t + 8)
            halves.append(_merge_pieces([y[g // 2][:, 128 * (g % 2):128 * (g % 2 + 1)] for g in groups],
                                        [i] * 8, [g % 8 for g in groups]))
        yi = jnp.concatenate(halves, axis=1)
        yi = jax.nn.gelu(yi + d_ref[...] * jnp.concatenate([ua[i], ub[i]], axis=1))
        out = yi * jax.nn.sigmoid(_dot(yi, gw_ref[...]) + gb_ref[...])
        oa_ref[pl.ds(i, nb, stride=sb), :] = out[:, 0:128]
        ob_ref[pl.ds(i, nb, stride=sb), :] = out[:, 128:256]
        if i % 2 == 1:
            yield


def _s5(za, zb, bst, tm, pm, pw_re, pw_im, d, gw, gb):
    B, Lp, _ = za.shape
    rows = _rows_per_step(B, S5_ROWS)
    nb = _pick_tile(Lp // S5_BLOCK, (264, 24, 16, 8))
    tk = nb * S5_BLOCK
    tok = pl.BlockSpec((rows, tk, 128), lambda b, j: (b, j, 0))
    small = (pw_re, pw_im, d, gw, gb)
    half = jax.ShapeDtypeStruct((B, Lp, 128), F32)
    return pl.pallas_call(
        _interleaved([(functools.partial(_s5_kernel, nb=nb), 2, 10, 2, 2)], rows),
        grid=(B // rows, Lp // tk),
        in_specs=[tok, tok] + [_const_spec(a) for a in (bst, tm, pm) + small],
        out_specs=[tok, tok],
        out_shape=[half, half],
        scratch_shapes=[pltpu.VMEM((rows, 8, S5_LANES), F32), pltpu.VMEM((rows, 8, S5_LANES), F32)],
        compiler_params=pltpu.CompilerParams(dimension_semantics=("parallel", "arbitrary"),
                                             vmem_limit_bytes=VMEM_LIMIT),
        name="s5",
    )(za, zb, bst, tm, pm, *small)


def _rope_tables(lp):
    half = HEAD_DIM // 2
    pos = jnp.arange(lp) - PAD_FRONT
    inv_freq = ROPE_BASE ** (-jnp.arange(half, dtype=F32) / half)
    ang = pos.astype(F32)[:, None] * inv_freq[None, :]
    cos, sin = jnp.cos(ang), jnp.sin(ang)
    cos_t = jnp.tile(jnp.concatenate([cos, cos], axis=1), (1, N_HEADS))
    sin_t = jnp.tile(jnp.concatenate([-sin, sin], axis=1), (1, N_HEADS))
    return cos_t, sin_t


def _retention_tables():
    C = CHUNK
    log_g = jnp.log1p(-jnp.exp2(-5.0 - jnp.arange(N_HEADS, dtype=F32)))
    t = jnp.arange(C, dtype=F32)
    rel = t[:, None] - t[None, :]
    causal = rel >= 0
    dm = jnp.where(causal[None], jnp.exp(jnp.where(causal, rel, 0.0)[None] * log_g[:, None, None]), 0.0)
    lane_g = jnp.repeat(log_g, HEAD_DIM)
    qdec = jnp.exp((t + 1.0)[:, None] * lane_g[None, :])
    kdec_t = jnp.exp(lane_g[:, None] * (C - 1.0 - t)[None, :])
    gcm = jnp.broadcast_to(jnp.exp(C * lane_g)[:, None], (GROUP_W, GROUP_W))
    return dm, qdec, kdec_t, gcm


def _s5_tables(a_re, a_im, log_dt, b_re, b_im, c_re, c_im):
    G, sb = S5_GROUPS, S5_BLOCK
    hi = lax.Precision.HIGHEST
    lam_re, lam_im = a_re.astype(F32), a_im.astype(F32)
    dt = jnp.exp(log_dt.astype(F32))[:, None]
    mag, ph = jnp.exp(lam_re * dt), lam_im * dt
    ab_re, ab_im = mag * jnp.cos(ph), mag * jnp.sin(ph)
    den = lam_re * lam_re + lam_im * lam_im
    nr, ni = ab_re - 1.0, ab_im
    zc_re = (nr * lam_re + ni * lam_im) / den
    zc_im = (ni * lam_re - nr * lam_im) / den
    b_re, b_im = b_re.astype(F32), b_im.astype(F32)
    bb_re = zc_re[..., None] * b_re - zc_im[..., None] * b_im
    bb_im = zc_re[..., None] * b_im + zc_im[..., None] * b_re
    c_re, c_im = c_re.astype(F32), c_im.astype(F32)
    eye = jnp.eye(G, dtype=F32)
    n = jnp.arange(sb + 1, dtype=F32)[:, None, None]
    mag_n, ph_n = jnp.exp(lam_re * dt * n), lam_im * dt * n
    pr, pi = mag_n * jnp.cos(ph_n), mag_n * jnp.sin(ph_n)

    e2 = jnp.eye(2, dtype=F32)
    npair, ch, st = G // 2, S5_CH, S5_STATE

    qr, qi = pr[sb - 1::-1, :, :, None], pi[sb - 1::-1, :, :, None]
    w = jnp.stack([qr * bb_re - qi * bb_im, qr * bb_im + qi * bb_re], axis=0)
    wq = jnp.transpose(w, (2, 1, 4, 0, 3)).reshape(npair, 2, sb, ch, 2, st)
    bst = (wq[:, :, :, :, :, None, :] * e2[None, :, None, None, None, :, None]).reshape(npair, GROUP_W, GROUP_W)

    m_re = c_re * pr[:, :, None, :] - c_im * pi[:, :, None, :]
    m_im = c_re * pi[:, :, None, :] + c_im * pr[:, :, None, :]
    coef = jnp.stack([m_re[1:], -m_im[1:]], axis=0)
    cq = jnp.transpose(coef, (2, 0, 4, 1, 3)).reshape(npair, 2, 2, st, sb, ch)
    cq = jnp.transpose(cq, (0, 2, 1, 3, 4, 5))
    pm = (cq[:, :, :, :, None, :, :] * e2[None, None, :, None, :, None, None]).reshape(npair, GROUP_W, GROUP_W)

    kern = (jnp.einsum('ngcp,gpd->ngcd', m_re[:sb], bb_re, precision=hi)
            - jnp.einsum('ngcp,gpd->ngcd', m_im[:sb], bb_im, precision=hi))
    lag = jnp.arange(sb)[None, :] - jnp.arange(sb)[:, None]
    k_ji = jnp.where((lag >= 0)[:, :, None, None, None], kern[jnp.clip(lag, 0, sb - 1)], 0.0)
    kq = jnp.transpose(k_ji, (2, 0, 4, 1, 3)).reshape(npair, 2, sb, ch, sb, ch)
    tm = (kq[:, :, :, :, None, :, :] * e2[None, :, None, None, :, None, None]).reshape(npair, GROUP_W, GROUP_W)

    lv = jnp.arange(16, dtype=F32)[:, None, None]
    m, p = jnp.exp(lam_re * dt * sb * 2.0 ** lv), lam_im * dt * sb * 2.0 ** lv
    pw_re, pw_im = (m * jnp.cos(p)).reshape(16, S5_LANES), (m * jnp.sin(p)).reshape(16, S5_LANES)
    return bst.astype(BF16), tm.astype(BF16), pm.astype(BF16), pw_re, pw_im


def _pick_tile(lp, candidates):
    for t in candidates:
        if lp % t == 0:
            return t
    return CHUNK


def kernel(x, meta_tokens, norm_mix_g, w_in, hgrn_lb_logits, hgrn_norm_g, rwkv_mu, rwkv_w0, rwkv_w_up,
           rwkv_a0, rwkv_a_up, rwkv_g_up, rwkv_k_k, rwkv_k_a, rwkv_r_k, rwkv_ln_w, rwkv_ln_b, rwkv_v0,
           rwkv_v_down, rwkv_v_up, s5_a_re, s5_a_im, s5_log_dt, s5_b_re, s5_b_im, s5_c_re, s5_c_im, s5_d,
           s5_glu_w, s5_glu_b, w_out, norm_ffn_g, w_ffn_up, w_ffn_down, norm_f_g):
    B, seq, _ = x.shape
    depth = w_in.shape[0]
    assert seq % CHUNK == 0
    lp = CHUNK + seq
    tm = _pick_tile(lp, (704, 384, 256, 128))
    tm_last = _pick_tile(seq, (512, 256, 128))
    row2 = lambda a: a.astype(F32).reshape(1, -1)

    meta = jnp.broadcast_to(meta_tokens.astype(x.dtype)[None], (B, N_META, D_MODEL))
    head = jnp.concatenate([jnp.zeros((B, PAD_FRONT, D_MODEL), x.dtype), meta], axis=1)
    if depth > 1:
        stream = (head, x)
    else:
        stream = (jnp.concatenate([head, x], axis=1),)

    p = jax.nn.softmax(hgrn_lb_logits.astype(F32), axis=0)
    lower_bounds = jnp.cumsum(p, axis=0) - p[0]
    cos_t, sin_t = _rope_tables(lp)
    dm, qdec, kdec_t, gcm = _retention_tables()

    def pad_rows(a, n):
        return jnp.pad(a, ((0, n - a.shape[0]), (0, 0)))

    v_first = None
    for l in range(depth):
        w = w_in[l]
        w_pad = jnp.concatenate([w[:, :2048 + RWKV_COLS], jnp.zeros((D_MODEL, RWKV_COLS_PAD - RWKV_COLS), w.dtype),
                                 w[:, 2048 + RWKV_COLS:]], axis=1).astype(BF16)
        z_ret, z_hgrn, z_rwkv, z_s5a, z_s5b = _in_proj(stream, lp, row2(norm_mix_g[l]), w_pad, tm)

        mu = jnp.pad(rwkv_mu[l].astype(F32), (0, RWKV_COLS_PAD - RWKV_COLS)).reshape(1, -1)
        w_up = pad_rows(rwkv_w_up[l], 128).astype(BF16)
        a_up = jnp.concatenate([jnp.zeros_like(rwkv_a_up[l]), rwkv_a_up[l]], axis=0).astype(BF16)
        g_up = pad_rows(rwkv_g_up[l], GROUP_W).astype(BF16)
        params = (mu, row2(rwkv_w0[l]), w_up, row2(rwkv_a0[l]), a_up, g_up, row2(rwkv_k_k[l]),
                  row2(rwkv_k_a[l]), row2(rwkv_r_k[l]), row2(rwkv_ln_w[l]), row2(rwkv_ln_b[l]))
        vmix = None
        if l > 0:
            v_down = jnp.pad(rwkv_v_down[l - 1], ((0, 0), (0, 128 - rwkv_v_down.shape[-1]))).astype(BF16)
            v_up = pad_rows(rwkv_v_up[l - 1], 128).astype(BF16)
            vmix = (row2(rwkv_v0[l - 1]), v_down, v_up)
        o_ret = _retention(z_ret, cos_t, sin_t, dm, qdec, kdec_t, gcm)
        o_hgrn = _hgrn2(z_hgrn, row2(lower_bounds[l]), row2(jnp.tile(hgrn_norm_g[l], N_HEADS)))
        o_rwkv, v_first = _rwkv7(z_rwkv, v_first, params, vmix)

        tabs = _s5_tables(s5_a_re[l], s5_a_im[l], s5_log_dt[l], s5_b_re[l], s5_b_im[l], s5_c_re[l], s5_c_im[l])
        o_s5a, o_s5b = _s5(z_s5a, z_s5b, *tabs, row2(s5_d[l]), s5_glu_w[l].astype(BF16), row2(s5_glu_b[l]))

        last = l == depth - 1
        h = _out_ffn(stream, lp, (o_ret, o_hgrn, o_rwkv, o_s5a, o_s5b), w_out[l].astype(BF16),
                     row2(norm_ffn_g[l]), w_ffn_up[l].astype(BF16), w_ffn_down[l].astype(BF16),
                     row2(norm_f_g), tm_last if last else tm, final=last)
        stream = (h,)
    return h
```

```python
import functools

import jax
import jax.numpy as jnp
from jax import lax
from jax.experimental import pallas as pl
from jax.experimental.pallas import tpu as pltpu

F32 = jnp.float32
BF16 = jnp.bfloat16

D_MODEL = 1024
N_META = 16
HEAD_DIM = 64
GROUP_W = 256
N_HEADS = 4
D_FF = 4096
ROPE_BASE = 10000.0
NORM_EPS = 1e-6
RWKV_LN_EPS = 64e-5
S5_GROUPS = 16
S5_CH = 16
S5_STATE = 64
S5_LANES = S5_GROUPS * S5_STATE
RWKV_COLS = 1056
RWKV_COLS_PAD = 1152
N_Z = 1024 + 1024 + RWKV_COLS_PAD + 256

CHUNK = 128
PAD_FRONT = CHUNK - N_META
VMEM_LIMIT = 56 * 1024 * 1024
RET_ROWS, GLA_ROWS, RWKV_ROWS, S5_ROWS = 8, 4, 8, 2
S5_BLOCK = 8


def _rows_per_step(batch, rows):
    while batch % rows:
        rows -= 1
    return rows


def _dot(a, b):
    return jnp.dot(a.astype(BF16), b.astype(BF16), preferred_element_type=F32)


def _cumsum_rows(x):
    n = x.shape[0]
    tri = (_iota((n, n), 0) >= _iota((n, n), 1)).astype(BF16)
    out, rest = None, x
    for _ in range(3):
        part = rest.astype(BF16)
        rest = rest - part.astype(F32)
        d = jnp.dot(tri, part, preferred_element_type=F32)
        out = d if out is None else out + d
    return out


def _iota(shape, dim):
    return lax.broadcasted_iota(jnp.int32, shape, dim)


def _block_diag_mask():
    return (_iota((GROUP_W, GROUP_W), 0) >> 6) == (_iota((GROUP_W, GROUP_W), 1) >> 6)


def _head_sum(x, bd_ones):
    return jnp.dot(x.astype(BF16), bd_ones, preferred_element_type=F32)


def _silu(x):
    return x * jax.nn.sigmoid(x)


def _interleaved(parts, rows):
    def kern(*refs):
        n_in, n_out = sum(p[2] for p in parts), sum(p[3] for p in parts)
        ins, outs, scrs = refs[:n_in], refs[n_in:n_in + n_out], refs[n_in + n_out:]

        @pl.when(pl.program_id(1) == 0)
        def _():
            for s in scrs:
                s[...] = jnp.zeros_like(s)

        live = []
        for bi in range(rows):
            i0 = o0 = s0 = 0
            for body, n_row_in, n_i, n_o, n_s in parts:
                p_in, p_out, p_scr = ins[i0:i0 + n_i], outs[o0:o0 + n_o], scrs[s0:s0 + n_s]
                i0, o0, s0 = i0 + n_i, o0 + n_o, s0 + n_s
                live.append(body(*[r.at[bi] for r in p_in[:n_row_in]], *p_in[n_row_in:],
                                 *[r.at[bi] for r in p_out], *[s.at[bi] for s in p_scr]))
        while live:
            for g in list(live):
                if next(g, "done") == "done":
                    live.remove(g)
    return kern


def _const_spec(a):
    return pl.BlockSpec(a.shape, lambda b, j: (0,) * a.ndim)


def _elem_spec(tm, n, offset):
    return pl.BlockSpec((pl.Element(1), pl.Element(tm), pl.Element(n)),
                        lambda b, j: (b, pl.multiple_of(offset(j), 128), 0))


def _stream_specs(first, tm):
    if not first:
        return [pl.BlockSpec((None, tm, D_MODEL), lambda b, j: (b, j, 0))]
    return [pl.BlockSpec((None, CHUNK, D_MODEL), lambda b, j: (b, 0, 0)),
            _elem_spec(tm, D_MODEL, lambda j: jnp.maximum(j * tm - CHUNK, 0))]


def _read_stream(first, refs):
    if not first:
        return refs[0][...], refs[1:]
    x = refs[1][0]
    tile0 = jnp.concatenate([refs[0][...], x[:x.shape[0] - CHUNK]], axis=0)
    return jnp.where(pl.program_id(1) == 0, tile0, x), refs[2:]


def _resident(shape):
    return pl.BlockSpec(shape, lambda b, j: (0,) * len(shape), pipeline_mode=pl.Buffered(1))


def _in_proj_kernel(*refs, tm, first):
    x, (g_ref, w_ref, zr_ref, zh_ref, zw_ref, zsa_ref, zsb_ref) = _read_stream(first, refs)
    ms = jnp.mean(x * x, axis=-1, keepdims=True)
    xn = x * lax.rsqrt(ms + NORM_EPS) * g_ref[...]
    row = pl.program_id(1) * tm + _iota((tm, 1), 0)
    xb = jnp.where(row < PAD_FRONT, 0.0, xn).astype(BF16)
    zr_ref[...] = jnp.dot(xb, w_ref[:, 0:1024], preferred_element_type=F32)
    zh_ref[...] = jnp.dot(xb, w_ref[:, 1024:2048], preferred_element_type=F32)
    zw_ref[...] = jnp.dot(xb, w_ref[:, 2048:2048 + RWKV_COLS_PAD], preferred_element_type=F32)
    zs = jnp.dot(xb, w_ref[:, 2048 + RWKV_COLS_PAD:N_Z], preferred_element_type=F32)
    zsa_ref[...] = zs[:, 0:128]
    zsb_ref[...] = zs[:, 128:256]


def _in_proj(stream, lp, g, w, tm):
    first = len(stream) == 2
    B = stream[0].shape[0]
    row_spec = lambda n: pl.BlockSpec((None, tm, n), lambda b, j: (b, j, 0))
    widths = (1024, 1024, RWKV_COLS_PAD, 128, 128)
    return pl.pallas_call(
        functools.partial(_in_proj_kernel, tm=tm, first=first),
        grid=(B, lp // tm),
        in_specs=_stream_specs(first, tm) + [_resident((1, D_MODEL)), _resident((D_MODEL, N_Z))],
        out_specs=[row_spec(n) for n in widths],
        out_shape=[jax.ShapeDtypeStruct((B, lp, n), F32) for n in widths],
        compiler_params=pltpu.CompilerParams(dimension_semantics=("parallel", "parallel"),
                                             vmem_limit_bytes=VMEM_LIMIT),
        name="in_proj",
    )(*stream, g, w)


def _ffn_kernel(*refs, final, first, fc):
    if final:
        h, refs = refs[0][0], refs[1:]
        mix = [r[0] for r in refs[:5]]
    else:
        h, refs = _read_stream(first, refs)
        mix = [r[...] for r in refs[:5]]
    wo_ref, g_ref, wu_ref, wd_ref, gf_ref, out_ref = refs[5:]
    o = jnp.concatenate(mix, axis=1).astype(BF16)
    acc = h + jnp.dot(o, wo_ref[...], preferred_element_type=F32)
    ms = jnp.mean(acc * acc, axis=-1, keepdims=True)
    hb = (acc * lax.rsqrt(ms + NORM_EPS) * g_ref[...]).astype(BF16)
    mlp = None
    for c in range(0, D_FF, fc):
        u = jnp.dot(hb, wu_ref[:, c:c + fc], preferred_element_type=F32)
        u = jnp.square(jnp.maximum(u, 0.0)).astype(BF16)
        d = jnp.dot(u, wd_ref[c:c + fc, :], preferred_element_type=F32)
        mlp = d if mlp is None else mlp + d
    y = acc + mlp
    if final:
        ms = jnp.mean(y * y, axis=-1, keepdims=True)
        y = y * lax.rsqrt(ms + NORM_EPS) * gf_ref[...]
    out_ref[...] = y


def _out_ffn(stream, lp, outs, wo, g, wu, wd, gf, tm, final):
    first = len(stream) == 2
    B = stream[0].shape[0]
    row_spec = lambda n: pl.BlockSpec((None, tm, n), lambda b, j: (b, j, 0))
    if final:
        rows_out = lp - CHUNK
        specs = [_elem_spec(tm, a.shape[-1], lambda j: CHUNK + j * tm) for a in (stream[0],) + tuple(outs)]
    else:
        rows_out = lp
        specs = _stream_specs(first, tm) + [row_spec(o.shape[-1]) for o in outs]
    return pl.pallas_call(
        functools.partial(_ffn_kernel, final=final, first=first, fc=1024),
        grid=(B, rows_out // tm),
        in_specs=specs + [_resident((D_MODEL, D_MODEL)), _resident((1, D_MODEL)), _resident((D_MODEL, D_FF)),
                          _resident((D_FF, D_MODEL)), _resident((1, D_MODEL))],
        out_specs=row_spec(D_MODEL),
        out_shape=jax.ShapeDtypeStruct((B, rows_out, D_MODEL), F32),
        compiler_params=pltpu.CompilerParams(dimension_semantics=("parallel", "parallel"),
                                             vmem_limit_bytes=VMEM_LIMIT),
        name="out_ffn",
    )(*stream, *outs, wo, g, wu, wd, gf)


def _ret_kernel(z_ref, cos_ref, sin_ref, dm_ref, qdec_ref, kdec_t_ref, gcm_ref, o_ref, s_scr):
    bdm = _block_diag_mask()
    bd_ones = bdm.astype(BF16)
    z = z_ref[...]
    q, k, v, g = z[:, 0:256], z[:, 256:512], z[:, 512:768], z[:, 768:1024]
    cos, sin = cos_ref[...], sin_ref[...]
    upper = (_iota((1, GROUP_W), 1) & (HEAD_DIM - 1)) >= HEAD_DIM // 2

    def rot(t):
        swapped = jnp.where(upper, pltpu.roll(t, HEAD_DIM // 2, 1),
                            pltpu.roll(t, GROUP_W - HEAD_DIM // 2, 1))
        return t * cos + swapped * sin

    qr = rot(q)
    kr = rot(k) * HEAD_DIM ** -0.5
    yield
    s = s_scr[...]
    o = _dot(qr * qdec_ref[...], s)
    vb = v.astype(BF16)
    kr_t = kr.T
    head_row = [(_iota((GROUP_W, 1), 0) >> 6) == h for h in range(N_HEADS)]
    head_lane = [(_iota((1, GROUP_W), 1) >> 6) == h for h in range(N_HEADS)]
    kr_tb = kr_t.astype(BF16)
    sc = jnp.dot(qr.astype(BF16), jnp.concatenate([jnp.where(hr, kr_tb, 0.0) for hr in head_row], axis=1),
                 preferred_element_type=F32)
    kv = _dot(kr_t * kdec_t_ref[...], vb)
    s_scr[...] = gcm_ref[...] * s + jnp.where(bdm, kv, 0.0)
    yield
    v_heads = jnp.concatenate([jnp.where(hl, vb, 0.0) for hl in head_lane], axis=0)
    o = o + jnp.dot((sc * dm_ref[...]).astype(BF16), v_heads, preferred_element_type=F32)
    yield
    msq = _head_sum(o * o, bd_ones) * (1.0 / HEAD_DIM)
    o_ref[...] = o * lax.rsqrt(msq + NORM_EPS) * _silu(g)


def _gla_kernel(z_ref, lb_ref, ng_ref, o_ref, s_scr):
    C = CHUNK
    bdm = _block_diag_mask()
    bd_ones = bdm.astype(BF16)
    z = z_ref[...]
    q, f, v, g =z[:, 0:256], z[:, 256:512], z[:, 512:768], z[:, 768:1024]
    lb = lb_ref[...]
    forget = lb + (1.0 - lb) * jax.nn.sigmoid(f)
    k = 1.0 - forget
    logf = jnp.log(forget)
    q = _silu(q) * HEAD_DIM ** -0.5
    row = _iota((C, 1), 0)
    ti, si = _iota((C, C), 0), _iota((C, C), 1)
    b = _cumsum_rows(logf)
    yield

    o = jnp.dot((q * k).astype(BF16), bd_ones, preferred_element_type=F32) * v

    head_lane = [(_iota((1, GROUP_W), 1) >> 6) == h for h in range(N_HEADS)]
    head_row = [(_iota((GROUP_W, 1), 0) >> 6) == h for h in range(N_HEADS)]
    b8 = b.reshape(C // 8, 8, GROUP_W)
    p = None
    m = 1
    while 2 * m <= C:
        if m == 1:
            ref = jnp.where((row & 1) == 1, pltpu.roll(b, 1, 0), b)
        elif m == 2:
            ref = jnp.where(_iota((1, 8, 1), 1) < 4, b8[:, 1:2, :], b8[:, 5:6, :]).reshape(C, GROUP_W)
        else:
            b3 = b.reshape(C // (2 * m), 2 * m, GROUP_W)
            ref = jnp.broadcast_to(b3[:, m - 1:m, :], b3.shape).reshape(C, GROUP_W)
        second = (row & (2 * m - 1)) >= m
        qm = jnp.where(second, q * jnp.exp(b - ref), 0.0).astype(BF16)
        km_t = jnp.where(second, 0.0, k * jnp.exp(ref - b)).T.astype(BF16)
        rk = jnp.concatenate([jnp.where(hr, km_t, 0.0) for hr in head_row], axis=1)
        sc = jnp.dot(qm, rk, preferred_element_type=F32)
        if 2 * m < C:
            sc = jnp.where(jnp.concatenate([(ti ^ si) < 2 * m] * N_HEADS, axis=1), sc, 0.0)
        p = sc if p is None else p + sc
        m *= 2
        yield
    vb = v.astype(BF16)
    v_heads = jnp.concatenate([jnp.where(hl, vb, 0.0) for hl in head_lane], axis=0)
    o = o + jnp.dot(p.astype(BF16), v_heads, preferred_element_type=F32)
    yield

    s = s_scr[...]
    o = o + _dot(q * jnp.exp(b), s)
    b_t = b.T
    b_last = b_t[:, C - 1:C]
    k_end = k.T * jnp.exp(b_last - b_t)
    s_scr[...] = jnp.exp(b_last) * s + jnp.where(bdm, _dot(k_end, vb), 0.0)
    yield
    msq = _head_sum(o * o, bd_ones) * (1.0 / HEAD_DIM)
    o_ref[...] = o * lax.rsqrt(msq + NORM_EPS) * ng_ref[...] * _silu(g)


def _rwkv_kernel(*refs, has_vmix):
    C = CHUNK
    if has_vmix:
        (z_ref, vf_ref, mu_ref, w0_ref, wup_ref, a0_ref, aup_ref, gup_ref, kk_ref, ka_ref, rk_ref,
         lnw_ref, lnb_ref, v0_ref, vdn_ref, vup_ref, o_ref, t_scr, carry_scr) = refs
    else:
        (z_ref, mu_ref, w0_ref, wup_ref, a0_ref, aup_ref, gup_ref, kk_ref, ka_ref, rk_ref,
         lnw_ref, lnb_ref, o_ref, vf_out_ref, t_scr, carry_scr) = refs

    bdm = _block_diag_mask()
    bd_ones = bdm.astype(BF16)
    row = _iota((C, 1), 0)
    ti, si = _iota((C, C), 0), _iota((C, C), 1)

    z = z_ref[...]
    z_prev = jnp.where(row == 0, carry_scr[0:1, :], pltpu.roll(z, 1, 0))
    carry_scr[0:1, :] = z[C - 1:C, :]
    zf = z + (z_prev - z) * mu_ref[...]
    r, k, v = zf[:, 0:256], zf[:, 256:512], zf[:, 512:768]
    wa, gl = zf[:, 768:896], zf[:, 896:RWKV_COLS_PAD]

    w_log = -jax.nn.softplus(-(w0_ref[...] + _dot(jnp.tanh(wa), wup_ref[...]))) - 0.5
    logw = -jnp.exp(w_log)
    a = jax.nn.sigmoid(a0_ref[...] + _dot(wa, aup_ref[...]))
    gate = _dot(jax.nn.sigmoid(gl), gup_ref[...])
    if has_vmix:
        mix = jax.nn.sigmoid(v0_ref[...] + _dot(_dot(v, vdn_ref[...]), vup_ref[...]))
        v = v + (vf_ref[...] - v) * mix
    else:
        vf_out_ref[...] = v
    kk = k * kk_ref[...]
    kk = kk / jnp.maximum(jnp.sqrt(_head_sum(kk * kk, bd_ones)), 1e-12)
    k = k * (1.0 + (a - 1.0) * ka_ref[...])
    alpha = -kk * a
    yield

    gc = _cumsum_rows(logw)
    yield
    gp = gc - logw
    gm = gc[C // 2:C // 2 + 1, :]
    r_mid, b_mid = r * jnp.exp(gc - gm), kk * jnp.exp(gp - gm)
    r_abs, b_abs = r * jnp.exp(gc), kk * jnp.exp(gp)
    g_t, a_t, k_t = gc.T, alpha.T, k.T
    from_mid = jnp.exp(g_t[:, C // 2:C // 2 + 1] - g_t)
    g_last = g_t[:, C - 1:C]
    to_end = jnp.exp(g_last - g_t)

    yield
    head_rows = [(_iota((GROUP_W, 1), 0) >> 6) == h for h in range(N_HEADS)]
    a_mid, k_mid = a_t * from_mid, k_t * from_mid
    rhs_t = jnp.concatenate([jnp.where(m, a_mid, 0.0) for m in head_rows]
                            + [jnp.where(m, k_mid, 0.0) for m in head_rows], axis=1)
    sc = _dot(jnp.concatenate([b_mid, r_mid], axis=0), rhs_t)
    yield

    t0 = t_scr[...]
    t0b = t0.astype(BF16)
    strict2 = jnp.concatenate([ti > si] * 2, axis=1)
    incl2 = jnp.concatenate([ti >= si] * 2, axis=1)
    pairs = range(N_HEADS // 2)
    head_lane = [(_iota((1, GROUP_W), 1) >> 6) == h for h in range(N_HEADS)]
    zeros_cc = jnp.zeros((C, C), BF16)

    def stack(x, p):
        xb = x.astype(BF16)
        return jnp.concatenate([jnp.where(head_lane[2 * p], xb, 0.0),
                                jnp.where(head_lane[2 * p + 1], xb, 0.0)], axis=0)

    def block_diag(xab):
        xb = xab.astype(BF16)
        return jnp.concatenate([jnp.concatenate([xb[:, 0:C], zeros_cc], axis=1),
                                jnp.concatenate([zeros_cc, xb[:, C:2 * C]], axis=1)], axis=0)

    v_stack = [stack(v, p) for p in pairs]
    rhs = jnp.dot(b_abs.astype(BF16), t0b, preferred_element_type=F32)
    for p in pairs:
        bm = jnp.where(strict2, sc[0:C, (N_HEADS + 2 * p) * C:(N_HEADS + 2 * p + 2) * C], 0.0)
        rhs = rhs + jnp.dot(bm.astype(BF16), v_stack[p], preferred_element_type=F32)
    yield

    a_ab = [jnp.where(strict2, sc[0:C, 2 * p * C:(2 * p + 2) * C], 0.0) for p in pairs]
    lower_left = lambda m: jnp.concatenate(
        [((ti & (2 * m - 1)) >= m) & ((si & (2 * m - 1)) < m) & ((ti ^ si) < 2 * m)] * 2, axis=1)
    eye2 = jnp.concatenate([ti == si] * 2, axis=1)
    x_ab = [jnp.where(eye2, 1.0, jnp.where(lower_left(1), a_ab[p], 0.0)) for p in pairs]
    for lvl in range(1, C.bit_length() - 1):
        sel = lower_left(1 << lvl)
        for p in pairs:
            y = jnp.dot(jnp.where(sel, a_ab[p], 0.0).astype(BF16), block_diag(x_ab[p]),
                        preferred_element_type=F32)
            x_ab[p] = x_ab[p] + jnp.dot(x_ab[p].astype(BF16), block_diag(y), preferred_element_type=F32)
        yield
    n_ab = [jnp.where(eye2, 0.0, x_ab[p]) for p in pairs]

    u = rhs + sum(jnp.dot(n_ab[p].astype(BF16), stack(rhs, p), preferred_element_type=F32) for p in pairs)
    yield
    o = jnp.dot(r_abs.astype(BF16), t0b, preferred_element_type=F32)
    for p in pairs:
        ao = jnp.where(incl2, sc[C:2 * C, 2 * p * C:(2 * p + 2) * C], 0.0)
        bo = jnp.where(incl2, sc[C:2 * C, (N_HEADS + 2 * p) * C:(N_HEADS + 2 * p + 2) * C], 0.0)
        o = o + jnp.dot(jnp.concatenate([ao, bo], axis=1).astype(BF16),
                        jnp.concatenate([stack(u, p), v_stack[p]], axis=0), preferred_element_type=F32)
    uv = jnp.concatenate([u, v], axis=0).astype(BF16)
    upd = _dot(jnp.concatenate([a_t * to_end, k_t * to_end], axis=1), uv)
    t_scr[...] = jnp.exp(g_last) * t0 + jnp.where(bdm, upd, 0.0)
    yield

    inv_d = 1.0 / HEAD_DIM
    dev = o - _head_sum(o, bd_ones) * inv_d
    var = _head_sum(dev * dev, bd_ones) * inv_d
    on = dev * lax.rsqrt(var + RWKV_LN_EPS) * lnw_ref[...] + lnb_ref[...]
    on = on + _head_sum(r * k * rk_ref[...], bd_ones) * v
    o_ref[...] = on * gate


def _chunk_mixer(body, rows_per_step, row_args, const_specs, const_args, n_out, scratch, name):
    B, Lp, _ = row_args[0].shape
    rows = _rows_per_step(B, rows_per_step)
    tok = lambda n: pl.BlockSpec((rows, CHUNK, n), lambda b, j: (b, j, 0))
    res = pl.pallas_call(
        _interleaved([(body, len(row_args), len(row_args) + len(const_args), n_out, len(scratch))], rows),
        grid=(B // rows, Lp // CHUNK),
        in_specs=[tok(a.shape[-1]) for a in row_args] + list(const_specs),
        out_specs=[tok(GROUP_W)] * n_out,
        out_shape=[jax.ShapeDtypeStruct((B, Lp, GROUP_W), F32)] * n_out,
        scratch_shapes=[pltpu.VMEM((rows,) + s, F32) for s in scratch],
        compiler_params=pltpu.CompilerParams(dimension_semantics=("parallel", "arbitrary")),
        name=name,
    )(*row_args, *const_args)
    return res


def _retention(z, cos, sin, dm, qdec, kdec_t, gcm):
    pos_spec = pl.BlockSpec((CHUNK, GROUP_W), lambda b, j: (j, 0))
    consts = (dm, qdec, kdec_t, gcm)
    return _chunk_mixer(_ret_kernel, RET_ROWS, [z], [pos_spec, pos_spec] + [_const_spec(a) for a in consts],
                        (cos, sin) + consts, 1, [(GROUP_W, GROUP_W)], "retention")[0]


def _hgrn2(z, lb, ng):
    return _chunk_mixer(_gla_kernel, GLA_ROWS, [z], [_const_spec(lb), _const_spec(ng)], (lb, ng), 1,
                        [(GROUP_W, GROUP_W)], "hgrn2")[0]


def _rwkv7(z, v_first, params, vmix):
    has_vmix = vmix is not None
    consts = tuple(params) + (tuple(vmix) if has_vmix else ())
    res = _chunk_mixer(functools.partial(_rwkv_kernel, has_vmix=has_vmix), RWKV_ROWS,
                       [z, v_first] if has_vmix else [z], [_const_spec(a) for a in consts], consts,
                       1 if has_vmix else 2, [(GROUP_W, GROUP_W), (8, RWKV_COLS_PAD)], "rwkv7")
    return (res[0], v_first) if has_vmix else (res[0], res[1])


def _merge_pieces(sources, src_piece, dst_piece):
    piece = _iota((1, 128), 1) >> 4
    out = None
    for x, s, d in zip(sources, src_piece, dst_piece):
        x = x if s == d else pltpu.roll(x, (16 * (d - s)) % 128, 1)
        out = x if out is None else jnp.where(piece == d, x, out)
    return out


def _s5_kernel(ua_ref, ub_ref, bst_ref, tm_ref, pm_ref, tab_re_ref, tab_im_ref, d_ref, gw_ref, gb_ref,
               oa_ref, ob_ref, c_re_scr, c_im_scr, *, nb):
    sb, n_pairs = S5_BLOCK, S5_GROUPS // 2
    ua = [ua_ref[pl.ds(i, nb, stride=sb), :] for i in range(sb)]
    ub = [ub_ref[pl.ds(i, nb, stride=sb), :] for i in range(sb)]
    tokens = list(range(sb))
    ucat = []
    for q in range(n_pairs):
        src = ua if q < n_pairs // 2 else ub
        ucat.append(jnp.concatenate(
            [_merge_pieces(src, [(2 * q + gg) % 8] * sb, tokens) for gg in range(2)], axis=1).astype(BF16))
    yield
    w = [jnp.dot(ucat[q], bst_ref[q], preferred_element_type=F32) for q in range(n_pairs)]
    re = jnp.concatenate([x[:, 0:128] for x in w], axis=1)
    im = jnp.concatenate([x[:, 128:256] for x in w], axis=1)
    yield
    row = _iota((nb, 1), 0)
    first = row == 0
    for sh in (1, 2, 4):
        ar, ai = tab_re_ref[sh - 1:sh, :], tab_im_ref[sh - 1:sh, :]
        inside = (row & 7) >= sh
        sre = jnp.where(inside, pltpu.roll(re, sh, 0), 0.0)
        sim = jnp.where(inside, pltpu.roll(im, sh, 0), 0.0)
        re, im = re + ar * sre - ai * sim, im + ar * sim + ai * sre
    yield
    c0r, c0i = c_re_scr[0:1, :], c_im_scr[0:1, :]
    cr, ci = c0r, c0i
    tr, tq = tab_re_ref[...], tab_im_ref[...]
    groups_re, groups_im = [], []
    for r in range(nb // 8):
        gr = re[8 * r:8 * r + 8] + tr * cr - tq * ci
        gi = im[8 * r:8 * r + 8] + tr * ci + tq * cr
        cr, ci = gr[7:8], gi[7:8]
        groups_re.append(gr)
        groups_im.append(gi)
    re, im = jnp.concatenate(groups_re, axis=0), jnp.concatenate(groups_im, axis=0)
    c_re_scr[0:1, :] = cr
    c_im_scr[0:1, :] = ci
    x_re = jnp.where(first, c0r, pltpu.roll(re, 1, 0)).astype(BF16)
    x_im = jnp.where(first, c0i, pltpu.roll(im, 1, 0)).astype(BF16)
    yield
    y = []
    for q in range(n_pairs):
        x_in = jnp.concatenate([x_re[:, 128 * q:128 * (q + 1)], x_im[:, 128 * q:128 * (q + 1)]], axis=1)
        y.append(jnp.dot(ucat[q], tm_ref[q], preferred_element_type=F32)
                 + jnp.dot(x_in, pm_ref[q], preferred_element_type=F32))
    yield
    for i in range(sb):
        halves = []
        for t in range(2):
            groups = range(8 * t, 8 * t + 8)
            halves.append(_merge_pieces([y[g // 2][:, 128 * (g % 2):128 * (g % 2 + 1)] for g in groups],
                                        [i] * 8, [g % 8 for g in groups]))
        yi = jnp.concatenate(halves, axis=1)
        yi = jax.nn.gelu(yi + d_ref[...] * jnp.concatenate([ua[i], ub[i]], axis=1))
        out = yi * jax.nn.sigmoid(_dot(yi, gw_ref[...]) + gb_ref[...])
        oa_ref[pl.ds(i, nb, stride=sb), :] = out[:, 0:128]
        ob_ref[pl.ds(i, nb, stride=sb), :] = out[:, 128:256]
        if i % 2 == 1:
            yield


def _s5(za, zb, bst, tm, pm, tab_re, tab_im, d, gw, gb):
    B, Lp, _ = za.shape
    rows = _rows_per_step(B, S5_ROWS)
    nb = _pick_tile(Lp // S5_BLOCK, (264, 24, 16, 8))
    tk = nb * S5_BLOCK
    tok = pl.BlockSpec((rows, tk, 128), lambda b, j: (b, j, 0))
    small = (tab_re, tab_im, d, gw, gb)
    half = jax.ShapeDtypeStruct((B, Lp, 128), F32)
    return pl.pallas_call(
        _interleaved([(functools.partial(_s5_kernel, nb=nb), 2, 10, 2, 2)], rows),
        grid=(B // rows, Lp // tk),
        in_specs=[tok, tok] + [_const_spec(a) for a in (bst, tm, pm) + small],
        out_specs=[tok, tok],
        out_shape=[half, half],
        scratch_shapes=[pltpu.VMEM((rows, 8, S5_LANES), F32), pltpu.VMEM((rows, 8, S5_LANES), F32)],
        compiler_params=pltpu.CompilerParams(dimension_semantics=("parallel", "arbitrary"),
                                             vmem_limit_bytes=VMEM_LIMIT),
        name="s5",
    )(za, zb, bst, tm, pm, *small)


def _rope_tables(lp):
    half = HEAD_DIM // 2
    pos = jnp.arange(lp) - PAD_FRONT
    inv_freq = ROPE_BASE ** (-jnp.arange(half, dtype=F32) / half)
    ang = pos.astype(F32)[:, None] * inv_freq[None, :]
    cos, sin = jnp.cos(ang), jnp.sin(ang)
    cos_t = jnp.tile(jnp.concatenate([cos, cos], axis=1), (1, N_HEADS))
    sin_t = jnp.tile(jnp.concatenate([-sin, sin], axis=1), (1, N_HEADS))
    return cos_t, sin_t


def _retention_tables():
    C = CHUNK
    log_g = jnp.log1p(-jnp.exp2(-5.0 - jnp.arange(N_HEADS, dtype=F32)))
    t = jnp.arange(C, dtype=F32)
    rel = t[:, None] - t[None, :]
    causal = rel >= 0
    dm = jnp.where(causal[None], jnp.exp(jnp.where(causal, rel, 0.0)[None] * log_g[:, None, None]), 0.0)
    lane_g = jnp.repeat(log_g, HEAD_DIM)
    qdec = jnp.exp((t + 1.0)[:, None] * lane_g[None, :])
    kdec_t = jnp.exp(lane_g[:, None] * (C - 1.0 - t)[None, :])
    gcm = jnp.broadcast_to(jnp.exp(C * lane_g)[:, None], (GROUP_W, GROUP_W))
    dm = jnp.transpose(dm, (1, 0, 2)).reshape(C, N_HEADS * C)
    return dm, qdec, kdec_t, gcm


def _s5_tables(a_re, a_im, log_dt, b_re, b_im, c_re, c_im):
    G, sb = S5_GROUPS, S5_BLOCK
    hi = lax.Precision.HIGHEST
    lam_re, lam_im = a_re.astype(F32), a_im.astype(F32)
    dt = jnp.exp(log_dt.astype(F32))[:, None]
    mag, ph = jnp.exp(lam_re * dt), lam_im * dt
    ab_re, ab_im = mag * jnp.cos(ph), mag * jnp.sin(ph)
    den = lam_re * lam_re + lam_im * lam_im
    nr, ni = ab_re - 1.0, ab_im
    zc_re = (nr * lam_re + ni * lam_im) / den
    zc_im = (ni * lam_re - nr * lam_im) / den
    b_re, b_im = b_re.astype(F32), b_im.astype(F32)
    bb_re = zc_re[..., None] * b_re - zc_im[..., None] * b_im
    bb_im = zc_re[..., None] * b_im + zc_im[..., None] * b_re
    c_re, c_im = c_re.astype(F32), c_im.astype(F32)
    eye = jnp.eye(G, dtype=F32)
    n = jnp.arange(sb + 1, dtype=F32)[:, None, None]
    mag_n, ph_n = jnp.exp(lam_re * dt * n), lam_im * dt * n
    pr, pi = mag_n * jnp.cos(ph_n), mag_n * jnp.sin(ph_n)

    e2 = jnp.eye(2, dtype=F32)
    npair, ch, st = G // 2, S5_CH, S5_STATE

    qr, qi = pr[sb - 1::-1, :, :, None], pi[sb - 1::-1, :, :, None]
    w = jnp.stack([qr * bb_re - qi * bb_im, qr * bb_im + qi * bb_re], axis=0)
    wq = jnp.transpose(w, (2, 1, 4, 0, 3)).reshape(npair, 2, sb, ch, 2, st)
    bst = (wq[:, :, :, :, :, None, :] * e2[None, :, None, None, None, :, None]).reshape(npair, GROUP_W, GROUP_W)

    m_re = c_re * pr[:, :, None, :] - c_im * pi[:, :, None, :]
    m_im = c_re * pi[:, :, None, :] + c_im * pr[:, :, None, :]
    coef = jnp.stack([m_re[1:], -m_im[1:]], axis=0)
    cq = jnp.transpose(coef, (2, 0, 4, 1, 3)).reshape(npair, 2, 2, st, sb, ch)
    cq = jnp.transpose(cq, (0, 2, 1, 3, 4, 5))
    pm = (cq[:, :, :, :, None, :, :] * e2[None, None, :, None, :, None, None]).reshape(npair, GROUP_W, GROUP_W)

    kern = (jnp.einsum('ngcp,gpd->ngcd', m_re[:sb], bb_re, precision=hi)
            - jnp.einsum('ngcp,gpd->ngcd', m_im[:sb], bb_im, precision=hi))
    lag = jnp.arange(sb)[None, :] - jnp.arange(sb)[:, None]
    k_ji = jnp.where((lag >= 0)[:, :, None, None, None], kern[jnp.clip(lag, 0, sb - 1)], 0.0)
    kq = jnp.transpose(k_ji, (2, 0, 4, 1, 3)).reshape(npair, 2, sb, ch, sb, ch)
    tm = (kq[:, :, :, :, None, :, :] * e2[None, :, None, None, :, None, None]).reshape(npair, GROUP_W, GROUP_W)

    steps = sb * jnp.arange(1, 9, dtype=F32)[:, None, None]
    m, p = jnp.exp(lam_re * dt * steps), lam_im * dt * steps
    tab_re, tab_im = (m * jnp.cos(p)).reshape(8, S5_LANES), (m * jnp.sin(p)).reshape(8, S5_LANES)
    return bst.astype(BF16), tm.astype(BF16), pm.astype(BF16), tab_re, tab_im


def _pick_tile(lp, candidates):
    for t in candidates:
        if lp % t == 0:
            return t
    return CHUNK


def kernel(x, meta_tokens, norm_mix_g, w_in, hgrn_lb_logits, hgrn_norm_g, rwkv_mu, rwkv_w0, rwkv_w_up,
           rwkv_a0, rwkv_a_up, rwkv_g_up, rwkv_k_k, rwkv_k_a, rwkv_r_k, rwkv_ln_w, rwkv_ln_b, rwkv_v0,
           rwkv_v_down, rwkv_v_up, s5_a_re, s5_a_im, s5_log_dt, s5_b_re, s5_b_im, s5_c_re, s5_c_im, s5_d,
           s5_glu_w, s5_glu_b, w_out, norm_ffn_g, w_ffn_up, w_ffn_down, norm_f_g):
    B, seq, _ = x.shape
    depth = w_in.shape[0]
    assert seq % CHUNK == 0
    lp = CHUNK + seq
    tm = _pick_tile(lp, (704, 384, 256, 128))
    tm_last = _pick_tile(seq, (512, 256, 128))
    row2 = lambda a: a.astype(F32).reshape(1, -1)

    meta = jnp.broadcast_to(meta_tokens.astype(x.dtype)[None], (B, N_META, D_MODEL))
    head = jnp.concatenate([jnp.zeros((B, PAD_FRONT, D_MODEL), x.dtype), meta], axis=1)
    if depth > 1:
        stream = (head, x)
    else:
        stream = (jnp.concatenate([head, x], axis=1),)

    p = jax.nn.softmax(hgrn_lb_logits.astype(F32), axis=0)
    lower_bounds = jnp.cumsum(p, axis=0) - p[0]
    cos_t, sin_t = _rope_tables(lp)
    dm, qdec, kdec_t, gcm = _retention_tables()

    def pad_rows(a, n):
        return jnp.pad(a, ((0, n - a.shape[0]), (0, 0)))

    v_first = None
    for l in range(depth):
        w = w_in[l].astype(BF16)
        w_pad = jnp.concatenate([w[:, :2048 + RWKV_COLS], jnp.zeros((D_MODEL, RWKV_COLS_PAD - RWKV_COLS), BF16),
                                 w[:, 2048 + RWKV_COLS:]], axis=1)
        z_ret, z_hgrn, z_rwkv, z_s5a, z_s5b = _in_proj(stream, lp, row2(norm_mix_g[l]), w_pad, tm)

        mu = jnp.pad(rwkv_mu[l].astype(F32), (0, RWKV_COLS_PAD - RWKV_COLS)).reshape(1, -1)
        w_up = pad_rows(rwkv_w_up[l], 128).astype(BF16)
        a_up = jnp.concatenate([jnp.zeros_like(rwkv_a_up[l]), rwkv_a_up[l]], axis=0).astype(BF16)
        g_up = pad_rows(rwkv_g_up[l], GROUP_W).astype(BF16)
        params = (mu, row2(rwkv_w0[l]), w_up, row2(rwkv_a0[l]), a_up, g_up, row2(rwkv_k_k[l]),
                  row2(rwkv_k_a[l]), row2(rwkv_r_k[l]), row2(rwkv_ln_w[l]), row2(rwkv_ln_b[l]))
        vmix = None
        if l > 0:
            v_down = jnp.pad(rwkv_v_down[l - 1], ((0, 0), (0, 128 - rwkv_v_down.shape[-1]))).astype(BF16)
            v_up = pad_rows(rwkv_v_up[l - 1], 128).astype(BF16)
            vmix = (row2(rwkv_v0[l - 1]), v_down, v_up)
        o_ret = _retention(z_ret, cos_t, sin_t, dm, qdec, kdec_t, gcm)
        o_hgrn = _hgrn2(z_hgrn, row2(lower_bounds[l]), row2(jnp.tile(hgrn_norm_g[l], N_HEADS)))
        o_rwkv, v_first = _rwkv7(z_rwkv, v_first, params, vmix)

        tabs = _s5_tables(s5_a_re[l], s5_a_im[l], s5_log_dt[l], s5_b_re[l], s5_b_im[l], s5_c_re[l], s5_c_im[l])
        o_s5a, o_s5b = _s5(z_s5a, z_s5b, *tabs, row2(s5_d[l]), s5_glu_w[l].astype(BF16), row2(s5_glu_b[l]))

        last = l == depth - 1
        h = _out_ffn(stream, lp, (o_ret, o_hgrn, o_rwkv, o_s5a, o_s5b), w_out[l].astype(BF16),
                     row2(norm_ffn_g[l]), w_ffn_up[l].astype(BF16), w_ffn_down[l].astype(BF16),
                     row2(norm_f_g), tm_last if last else tm, final=last)
        stream = (h,)
    return h
```

```python
import functools

import jax
import jax.numpy as jnp
from jax import lax
from jax.experimental import pallas as pl
from jax.experimental.pallas import tpu as pltpu

F32 = jnp.float32
BF16 = jnp.bfloat16

D_MODEL = 1024
N_META = 16
HEAD_DIM = 64
GROUP_W = 256
N_HEADS = 4
D_FF = 4096
ROPE_BASE = 10000.0
NORM_EPS = 1e-6
RWKV_LN_EPS = 64e-5
S5_GROUPS = 16
S5_CH = 16
S5_STATE = 64
S5_LANES = S5_GROUPS * S5_STATE
RWKV_COLS = 1056
RWKV_COLS_PAD = 1152
N_IN = 1024 + 1024 + RWKV_COLS + 256

CHUNK = 128
PAD_FRONT = CHUNK - N_META
VMEM_LIMIT = 56 * 1024 * 1024
RET_ROWS, GLA_ROWS, RWKV_ROWS, S5_ROWS = 8, 4, 8, 2
S5_BLOCK = 8


def _rows_per_step(batch, rows):
    while batch % rows:
        rows -= 1
    return rows


def _dot(a, b):
    return jnp.dot(a.astype(BF16), b.astype(BF16), preferred_element_type=F32)


def _cumsum_rows(x):
    n = x.shape[0]
    tri = (_iota((n, n), 0) >= _iota((n, n), 1)).astype(BF16)
    out, rest = None, x
    for _ in range(3):
        part = rest.astype(BF16)
        rest = rest - part.astype(F32)
        d = jnp.dot(tri, part, preferred_element_type=F32)
        out = d if out is None else out + d
    return out


def _iota(shape, dim):
    return lax.broadcasted_iota(jnp.int32, shape, dim)


def _block_diag_mask():
    return (_iota((GROUP_W, GROUP_W), 0) >> 6) == (_iota((GROUP_W, GROUP_W), 1) >> 6)


def _head_sum(x, bd_ones):
    return jnp.dot(x.astype(BF16), bd_ones, preferred_element_type=F32)


def _silu(x):
    return x * jax.nn.sigmoid(x)


def _interleaved(parts, rows):
    def kern(*refs):
        n_in, n_out = sum(p[2] for p in parts), sum(p[3] for p in parts)
        ins, outs, scrs = refs[:n_in], refs[n_in:n_in + n_out], refs[n_in + n_out:]

        @pl.when(pl.program_id(1) == 0)
        def _():
            for s in scrs:
                s[...] = jnp.zeros_like(s)

        live = []
        for bi in range(rows):
            i0 = o0 = s0 = 0
            for body, n_row_in, n_i, n_o, n_s in parts:
                p_in, p_out, p_scr = ins[i0:i0 + n_i], outs[o0:o0 + n_o], scrs[s0:s0 + n_s]
                i0, o0, s0 = i0 + n_i, o0 + n_o, s0 + n_s
                live.append(body(*[r.at[bi] for r in p_in[:n_row_in]], *p_in[n_row_in:],
                                 *[r.at[bi] for r in p_out], *[s.at[bi] for s in p_scr]))
        while live:
            for g in list(live):
                if next(g, "done") == "done":
                    live.remove(g)
    return kern


def _const_spec(a):
    return pl.BlockSpec(a.shape, lambda b, j: (0,) * a.ndim)


def _elem_spec(tm, n, offset):
    return pl.BlockSpec((pl.Element(1), pl.Element(tm), pl.Element(n)),
                        lambda b, j: (b, pl.multiple_of(offset(j), 128), 0))


def _stream_specs(first, tm):
    if not first:
        return [pl.BlockSpec((None, tm, D_MODEL), lambda b, j: (b, j, 0))]
    return [pl.BlockSpec((None, CHUNK, D_MODEL), lambda b, j: (b, 0, 0)),
            _elem_spec(tm, D_MODEL, lambda j: jnp.maximum(j * tm - CHUNK, 0))]


def _read_stream(first, refs):
    if not first:
        return refs[0][...], refs[1:]
    x = refs[1][0]
    tile0 = jnp.concatenate([refs[0][...], x[:x.shape[0] - CHUNK]], axis=0)
    return jnp.where(pl.program_id(1) == 0, tile0, x), refs[2:]


def _resident(shape):
    return pl.BlockSpec(shape, lambda b, j: (0,) * len(shape), pipeline_mode=pl.Buffered(1))


def _in_proj_kernel(*refs, tm, first):
    x, (g_ref, w_ref, zr_ref, zh_ref, zw_ref, zsa_ref, zsb_ref) = _read_stream(first, refs)
    ms = jnp.mean(x * x, axis=-1, keepdims=True)
    xn = x * lax.rsqrt(ms + NORM_EPS) * g_ref[...]
    row = pl.program_id(1) * tm + _iota((tm, 1), 0)
    xb = jnp.where(row < PAD_FRONT, 0.0, xn).astype(BF16)
    zr_ref[...] = jnp.dot(xb, w_ref[:, 0:1024], preferred_element_type=F32)
    zh_ref[...] = jnp.dot(xb, w_ref[:, 1024:2048], preferred_element_type=F32)
    zw_ref[...] = jnp.dot(xb, w_ref[:, 2048:2048 + RWKV_COLS_PAD], preferred_element_type=F32)
    zs = jnp.dot(xb, w_ref[:, 2048 + RWKV_COLS:N_IN], preferred_element_type=F32)
    zsa_ref[...] = zs[:, 0:128]
    zsb_ref[...] = zs[:, 128:256]


def _in_proj(stream, lp, g, w, tm):
    first = len(stream) == 2
    B = stream[0].shape[0]
    row_spec = lambda n: pl.BlockSpec((None, tm, n), lambda b, j: (b, j, 0))
    widths = (1024, 1024, RWKV_COLS_PAD, 128, 128)
    return pl.pallas_call(
        functools.partial(_in_proj_kernel, tm=tm, first=first),
        grid=(B, lp // tm),
        in_specs=_stream_specs(first, tm) + [_resident((1, D_MODEL)), _resident((D_MODEL, N_IN))],
        out_specs=[row_spec(n) for n in widths],
        out_shape=[jax.ShapeDtypeStruct((B, lp, n), F32) for n in widths],
        compiler_params=pltpu.CompilerParams(dimension_semantics=("parallel", "parallel"),
                                             vmem_limit_bytes=VMEM_LIMIT),
        name="in_proj",
    )(*stream, g, w)


def _ffn_kernel(*refs, final, first, fc):
    if final:
        h, refs = refs[0][0], refs[1:]
        mix = [r[0] for r in refs[:5]]
    else:
        h, refs = _read_stream(first, refs)
        mix = [r[...] for r in refs[:5]]
    wo_ref, g_ref, wu_ref, wd_ref, gf_ref, out_ref = refs[5:]
    o = jnp.concatenate(mix, axis=1).astype(BF16)
    acc = h + jnp.dot(o, wo_ref[...], preferred_element_type=F32)
    ms = jnp.mean(acc * acc, axis=-1, keepdims=True)
    hb = (acc * lax.rsqrt(ms + NORM_EPS) * g_ref[...]).astype(BF16)
    mlp = None
    for c in range(0, D_FF, fc):
        u = jnp.dot(hb, wu_ref[:, c:c + fc], preferred_element_type=F32)
        u = jnp.square(jnp.maximum(u, 0.0)).astype(BF16)
        d = jnp.dot(u, wd_ref[c:c + fc, :], preferred_element_type=F32)
        mlp = d if mlp is None else mlp + d
    y = acc + mlp
    if final:
        ms = jnp.mean(y * y, axis=-1, keepdims=True)
        y = y * lax.rsqrt(ms + NORM_EPS) * gf_ref[...]
    out_ref[...] = y


def _out_ffn(stream, lp, outs, wo, g, wu, wd, gf, tm, final):
    first = len(stream) == 2
    B = stream[0].shape[0]
    row_spec = lambda n: pl.BlockSpec((None, tm, n), lambda b, j: (b, j, 0))
    if final:
        rows_out = lp - CHUNK
        specs = [_elem_spec(tm, a.shape[-1], lambda j: CHUNK + j * tm) for a in (stream[0],) + tuple(outs)]
    else:
        rows_out = lp
        specs = _stream_specs(first, tm) + [row_spec(o.shape[-1]) for o in outs]
    return pl.pallas_call(
        functools.partial(_ffn_kernel, final=final, first=first, fc=1024),
        grid=(B, rows_out // tm),
        in_specs=specs + [_resident((D_MODEL, D_MODEL)), _resident((1, D_MODEL)), _resident((D_MODEL, D_FF)),
                          _resident((D_FF, D_MODEL)), _resident((1, D_MODEL))],
        out_specs=row_spec(D_MODEL),
        out_shape=jax.ShapeDtypeStruct((B, rows_out, D_MODEL), F32),
        compiler_params=pltpu.CompilerParams(dimension_semantics=("parallel", "parallel"),
                                             vmem_limit_bytes=VMEM_LIMIT),
        name="out_ffn",
    )(*stream, *outs, wo, g, wu, wd, gf)


def _ret_kernel(z_ref, cos_ref, sin_ref, dm_ref, qdec_ref, kdec_t_ref, gcm_ref, o_ref, s_scr):
    bdm = _block_diag_mask()
    bd_ones = bdm.astype(BF16)
    z = z_ref[...]
    q, k, v, g = z[:, 0:256], z[:, 256:512], z[:, 512:768], z[:, 768:1024]
    cos, sin = cos_ref[...], sin_ref[...]
    upper = (_iota((1, GROUP_W), 1) & (HEAD_DIM - 1)) >= HEAD_DIM // 2

    def rot(t):
        swapped = jnp.where(upper, pltpu.roll(t, HEAD_DIM // 2, 1),
                            pltpu.roll(t, GROUP_W - HEAD_DIM // 2, 1))
        return t * cos + swapped * sin

    qr = rot(q)
    kr = rot(k) * HEAD_DIM ** -0.5
    yield
    s = s_scr[...]
    o = _dot(qr * qdec_ref[...], s)
    vb = v.astype(BF16)
    kr_t = kr.T
    head_row = [(_iota((GROUP_W, 1), 0) >> 6) == h for h in range(N_HEADS)]
    head_lane = [(_iota((1, GROUP_W), 1) >> 6) == h for h in range(N_HEADS)]
    kr_tb = kr_t.astype(BF16)
    sc = jnp.dot(qr.astype(BF16), jnp.concatenate([jnp.where(hr, kr_tb, 0.0) for hr in head_row], axis=1),
                 preferred_element_type=F32)
    kv = _dot(kr_t * kdec_t_ref[...], vb)
    s_scr[...] = gcm_ref[...] * s + jnp.where(bdm, kv, 0.0)
    yield
    v_heads = jnp.concatenate([jnp.where(hl, vb, 0.0) for hl in head_lane], axis=0)
    o = o + jnp.dot((sc * dm_ref[...]).astype(BF16), v_heads, preferred_element_type=F32)
    yield
    msq = _head_sum(o * o, bd_ones) * (1.0 / HEAD_DIM)
    o_ref[...] = o * lax.rsqrt(msq + NORM_EPS) * _silu(g)


def _gla_kernel(z_ref, lb_ref, ng_ref, o_ref, s_scr):
    C = CHUNK
    bdm = _block_diag_mask()
    bd_ones = bdm.astype(BF16)
    z = z_ref[...]
    q, f, v, g =z[:, 0:256], z[:, 256:512], z[:, 512:768], z[:, 768:1024]
    lb = lb_ref[...]
    forget = lb + (1.0 - lb) * jax.nn.sigmoid(f)
    k = 1.0 - forget
    logf = jnp.log(forget)
    q = _silu(q) * HEAD_DIM ** -0.5
    row = _iota((C, 1), 0)
    ti, si = _iota((C, C), 0), _iota((C, C), 1)
    b = _cumsum_rows(logf)
    yield

    o = jnp.dot((q * k).astype(BF16), bd_ones, preferred_element_type=F32) * v

    head_lane = [(_iota((1, GROUP_W), 1) >> 6) == h for h in range(N_HEADS)]
    head_row = [(_iota((GROUP_W, 1), 0) >> 6) == h for h in range(N_HEADS)]
    b8 = b.reshape(C // 8, 8, GROUP_W)
    p = None
    m = 1
    while 2 * m <= C:
        if m == 1:
            ref = jnp.where((row & 1) == 1, pltpu.roll(b, 1, 0), b)
        elif m == 2:
            ref = jnp.where(_iota((1, 8, 1), 1) < 4, b8[:, 1:2, :], b8[:, 5:6, :]).reshape(C, GROUP_W)
        else:
            b3 = b.reshape(C // (2 * m), 2 * m, GROUP_W)
            ref = jnp.broadcast_to(b3[:, m - 1:m, :], b3.shape).reshape(C, GROUP_W)
        second = (row & (2 * m - 1)) >= m
        qm = jnp.where(second, q * jnp.exp(b - ref), 0.0).astype(BF16)
        km_t = jnp.where(second, 0.0, k * jnp.exp(ref - b)).T.astype(BF16)
        rk = jnp.concatenate([jnp.where(hr, km_t, 0.0) for hr in head_row], axis=1)
        sc = jnp.dot(qm, rk, preferred_element_type=F32)
        if 2 * m < C:
            sc = jnp.where(jnp.concatenate([(ti ^ si) < 2 * m] * N_HEADS, axis=1), sc, 0.0)
        p = sc if p is None else p + sc
        m *= 2
        yield
    vb = v.astype(BF16)
    v_heads = jnp.concatenate([jnp.where(hl, vb, 0.0) for hl in head_lane], axis=0)
    o = o + jnp.dot(p.astype(BF16), v_heads, preferred_element_type=F32)
    yield

    s = s_scr[...]
    o = o + _dot(q * jnp.exp(b), s)
    b_t = b.T
    b_last = b_t[:, C - 1:C]
    k_end = k.T * jnp.exp(b_last - b_t)
    s_scr[...] = jnp.exp(b_last) * s + jnp.where(bdm, _dot(k_end, vb), 0.0)
    yield
    msq = _head_sum(o * o, bd_ones) * (1.0 / HEAD_DIM)
    o_ref[...] = o * lax.rsqrt(msq + NORM_EPS) * ng_ref[...] * _silu(g)


def _rwkv_kernel(*refs, has_vmix):
    C = CHUNK
    if has_vmix:
        (z_ref, vf_ref, mu_ref, w0_ref, wup_ref, a0_ref, aup_ref, gup_ref, kk_ref, ka_ref, rk_ref,
         lnw_ref, lnb_ref, v0_ref, vdn_ref, vup_ref, o_ref, t_scr, carry_scr) = refs
    else:
        (z_ref, mu_ref, w0_ref, wup_ref, a0_ref, aup_ref, gup_ref, kk_ref, ka_ref, rk_ref,
         lnw_ref, lnb_ref, o_ref, vf_out_ref, t_scr, carry_scr) = refs

    bdm = _block_diag_mask()
    bd_ones = bdm.astype(BF16)
    row = _iota((C, 1), 0)
    ti, si = _iota((C, C), 0), _iota((C, C), 1)

    z = z_ref[...]
    z_prev = jnp.where(row == 0, carry_scr[0:1, :], pltpu.roll(z, 1, 0))
    carry_scr[0:1, :] = z[C - 1:C, :]
    zf = z + (z_prev - z) * mu_ref[...]
    r, k, v = zf[:, 0:256], zf[:, 256:512], zf[:, 512:768]
    wa, gl = zf[:, 768:896], zf[:, 896:RWKV_COLS_PAD]

    w_log = -jax.nn.softplus(-(w0_ref[...] + _dot(jnp.tanh(wa), wup_ref[...]))) - 0.5
    logw = -jnp.exp(w_log)
    a = jax.nn.sigmoid(a0_ref[...] + _dot(wa, aup_ref[...]))
    gate = _dot(jax.nn.sigmoid(gl), gup_ref[...])
    if has_vmix:
        mix = jax.nn.sigmoid(v0_ref[...] + _dot(_dot(v, vdn_ref[...]), vup_ref[...]))
        v = v + (vf_ref[...] - v) * mix
    else:
        vf_out_ref[...] = v
    kk = k * kk_ref[...]
    kk = kk / jnp.maximum(jnp.sqrt(_head_sum(kk * kk, bd_ones)), 1e-12)
    k = k * (1.0 + (a - 1.0) * ka_ref[...])
    alpha = -kk * a
    yield

    gc = _cumsum_rows(logw)
    yield
    gp = gc - logw
    gm = gc[C // 2:C // 2 + 1, :]
    r_mid, b_mid = r * jnp.exp(gc - gm), kk * jnp.exp(gp - gm)
    r_abs, b_abs = r * jnp.exp(gc), kk * jnp.exp(gp)
    g_t, a_t, k_t = gc.T, alpha.T, k.T
    from_mid = jnp.exp(g_t[:, C // 2:C // 2 + 1] - g_t)
    g_last = g_t[:, C - 1:C]
    to_end = jnp.exp(g_last - g_t)

    yield
    head_rows = [(_iota((GROUP_W, 1), 0) >> 6) == h for h in range(N_HEADS)]
    a_mid, k_mid = a_t * from_mid, k_t * from_mid
    rhs_t = jnp.concatenate([jnp.where(m, a_mid, 0.0) for m in head_rows]
                            + [jnp.where(m, k_mid, 0.0) for m in head_rows], axis=1)
    sc = _dot(jnp.concatenate([b_mid, r_mid], axis=0), rhs_t)
    yield

    t0 = t_scr[...]
    t0b = t0.astype(BF16)
    strict2 = jnp.concatenate([ti > si] * 2, axis=1)
    incl2 = jnp.concatenate([ti >= si] * 2, axis=1)
    pairs = range(N_HEADS // 2)
    head_lane = [(_iota((1, GROUP_W), 1) >> 6) == h for h in range(N_HEADS)]
    zeros_cc = jnp.zeros((C, C), BF16)

    def stack(x, p):
        xb = x.astype(BF16)
        return jnp.concatenate([jnp.where(head_lane[2 * p], xb, 0.0),
                                jnp.where(head_lane[2 * p + 1], xb, 0.0)], axis=0)

    def block_diag(xab):
        xb = xab.astype(BF16)
        return jnp.concatenate([jnp.concatenate([xb[:, 0:C], zeros_cc], axis=1),
                                jnp.concatenate([zeros_cc, xb[:, C:2 * C]], axis=1)], axis=0)

    v_stack = [stack(v, p) for p in pairs]
    rhs = jnp.dot(b_abs.astype(BF16), t0b, preferred_element_type=F32)
    for p in pairs:
        bm = jnp.where(strict2, sc[0:C, (N_HEADS + 2 * p) * C:(N_HEADS + 2 * p + 2) * C], 0.0)
        rhs = rhs + jnp.dot(bm.astype(BF16), v_stack[p], preferred_element_type=F32)
    yield

    a_ab = [jnp.where(strict2, sc[0:C, 2 * p * C:(2 * p + 2) * C], 0.0) for p in pairs]
    lower_left = lambda m: jnp.concatenate(
        [((ti & (2 * m - 1)) >= m) & ((si & (2 * m - 1)) < m) & ((ti ^ si) < 2 * m)] * 2, axis=1)
    eye2 = jnp.concatenate([ti == si] * 2, axis=1)
    x_ab = [jnp.where(eye2, 1.0, jnp.where(lower_left(1), a_ab[p], 0.0)) for p in pairs]
    for lvl in range(1, C.bit_length() - 1):
        sel = lower_left(1 << lvl)
        for p in pairs:
            y = jnp.dot(jnp.where(sel, a_ab[p], 0.0).astype(BF16), block_diag(x_ab[p]),
                        preferred_element_type=F32)
            x_ab[p] = x_ab[p] + jnp.dot(x_ab[p].astype(BF16), block_diag(y), preferred_element_type=F32)
        yield
    n_ab = [jnp.where(eye2, 0.0, x_ab[p]) for p in pairs]

    u = rhs + sum(jnp.dot(n_ab[p].astype(BF16), stack(rhs, p), preferred_element_type=F32) for p in pairs)
    yield
    o = jnp.dot(r_abs.astype(BF16), t0b, preferred_element_type=F32)
    for p in pairs:
        ao = jnp.where(incl2, sc[C:2 * C, 2 * p * C:(2 * p + 2) * C], 0.0)
        bo = jnp.where(incl2, sc[C:2 * C, (N_HEADS + 2 * p) * C:(N_HEADS + 2 * p + 2) * C], 0.0)
        o = o + jnp.dot(jnp.concatenate([ao, bo], axis=1).astype(BF16),
                        jnp.concatenate([stack(u, p), v_stack[p]], axis=0), preferred_element_type=F32)
    uv = jnp.concatenate([u, v], axis=0).astype(BF16)
    upd = _dot(jnp.concatenate([a_t * to_end, k_t * to_end], axis=1), uv)
    t_scr[...] = jnp.exp(g_last) * t0 + jnp.where(bdm, upd, 0.0)
    yield

    inv_d = 1.0 / HEAD_DIM
    dev = o - _head_sum(o, bd_ones) * inv_d
    var = _head_sum(dev * dev, bd_ones) * inv_d
    on = dev * lax.rsqrt(var + RWKV_LN_EPS) * lnw_ref[...] + lnb_ref[...]
    on = on + _head_sum(r * k * rk_ref[...], bd_ones) * v
    o_ref[...] = on * gate


def _chunk_mixer(body, rows_per_step, row_args, const_specs, const_args, n_out, scratch, name):
    B, Lp, _ = row_args[0].shape
    rows = _rows_per_step(B, rows_per_step)
    tok = lambda n: pl.BlockSpec((rows, CHUNK, n), lambda b, j: (b, j, 0))
    res = pl.pallas_call(
        _interleaved([(body, len(row_args), len(row_args) + len(const_args), n_out, len(scratch))], rows),
        grid=(B // rows, Lp // CHUNK),
        in_specs=[tok(a.shape[-1]) for a in row_args] + list(const_specs),
        out_specs=[tok(GROUP_W)] * n_out,
        out_shape=[jax.ShapeDtypeStruct((B, Lp, GROUP_W), F32)] * n_out,
        scratch_shapes=[pltpu.VMEM((rows,) + s, F32) for s in scratch],
        compiler_params=pltpu.CompilerParams(dimension_semantics=("parallel", "arbitrary")),
        name=name,
    )(*row_args, *const_args)
    return res


def _retention(z, cos, sin, dm, qdec, kdec_t, gcm):
    pos_spec = pl.BlockSpec((CHUNK, GROUP_W), lambda b, j: (j, 0))
    consts = (dm, qdec, kdec_t, gcm)
    return _chunk_mixer(_ret_kernel, RET_ROWS, [z], [pos_spec, pos_spec] + [_const_spec(a) for a in consts],
                        (cos, sin) + consts, 1, [(GROUP_W, GROUP_W)], "retention")[0]


def _hgrn2(z, lb, ng):
    return _chunk_mixer(_gla_kernel, GLA_ROWS, [z], [_const_spec(lb), _const_spec(ng)], (lb, ng), 1,
                        [(GROUP_W, GROUP_W)], "hgrn2")[0]


def _rwkv7(z, v_first, params, vmix):
    has_vmix = vmix is not None
    consts = tuple(params) + (tuple(vmix) if has_vmix else ())
    res = _chunk_mixer(functools.partial(_rwkv_kernel, has_vmix=has_vmix), RWKV_ROWS,
                       [z, v_first] if has_vmix else [z], [_const_spec(a) for a in consts], consts,
                       1 if has_vmix else 2, [(GROUP_W, GROUP_W), (8, RWKV_COLS_PAD)], "rwkv7")
    return (res[0], v_first) if has_vmix else (res[0], res[1])


def _merge_pieces(sources, src_piece, dst_piece):
    piece = _iota((1, 128), 1) >> 4
    out = None
    for x, s, d in zip(sources, src_piece, dst_piece):
        x = x if s == d else pltpu.roll(x, (16 * (d - s)) % 128, 1)
        out = x if out is None else jnp.where(piece == d, x, out)
    return out


def _s5_kernel(ua_ref, ub_ref, bst_ref, tm_ref, pm_ref, tab_re_ref, tab_im_ref, d_ref, gw_ref, gb_ref,
               oa_ref, ob_ref, c_re_scr, c_im_scr, *, nb):
    sb, n_pairs = S5_BLOCK, S5_GROUPS // 2
    ua = [ua_ref[pl.ds(i, nb, stride=sb), :] for i in range(sb)]
    ub = [ub_ref[pl.ds(i, nb, stride=sb), :] for i in range(sb)]
    tokens = list(range(sb))
    ucat = []
    for q in range(n_pairs):
        src = ua if q < n_pairs // 2 else ub
        ucat.append(jnp.concatenate(
            [_merge_pieces(src, [(2 * q + gg) % 8] * sb, tokens) for gg in range(2)], axis=1).astype(BF16))
    yield
    z64, z128, half = jnp.zeros((128, 64), BF16), jnp.zeros((128, 128), BF16), S5_STATE
    bst, tmm, pmm = [], [], []
    for q in range(n_pairs):
        b0, b1 = bst_ref[2 * q], bst_ref[2 * q + 1]
        bst.append(jnp.concatenate([
            jnp.concatenate([b0[:, 0:half], z64, b0[:, half:], z64], axis=1),
            jnp.concatenate([z64, b1[:, 0:half], z64, b1[:, half:]], axis=1)], axis=0))
        tmm.append(jnp.concatenate([jnp.concatenate([tm_ref[2 * q], z128], axis=1),
                                    jnp.concatenate([z128, tm_ref[2 * q + 1]], axis=1)], axis=0))
        p0, p1 = pm_ref[2 * q], pm_ref[2 * q + 1]
        pmm.append(jnp.concatenate([
            jnp.concatenate([p0[0:half], z128[0:half]], axis=1), jnp.concatenate([z128[0:half], p1[0:half]], axis=1),
            jnp.concatenate([p0[half:], z128[0:half]], axis=1), jnp.concatenate([z128[0:half], p1[half:]], axis=1)],
            axis=0))
    w = [jnp.dot(ucat[q], bst[q], preferred_element_type=F32) for q in range(n_pairs)]
    re = jnp.concatenate([x[:, 0:128] for x in w], axis=1)
    im = jnp.concatenate([x[:, 128:256] for x in w], axis=1)
    yield
    row = _iota((nb, 1), 0)
    first = row == 0
    for sh in (1, 2, 4):
        ar, ai = tab_re_ref[sh - 1:sh, :], tab_im_ref[sh - 1:sh, :]
        inside = (row & 7) >= sh
        sre = jnp.where(inside, pltpu.roll(re, sh, 0), 0.0)
        sim = jnp.where(inside, pltpu.roll(im, sh, 0), 0.0)
        re, im = re + ar * sre - ai * sim, im + ar * sim + ai * sre
    yield
    c0r, c0i = c_re_scr[0:1, :], c_im_scr[0:1, :]
    cr, ci = c0r, c0i
    tr, tq = tab_re_ref[...], tab_im_ref[...]
    groups_re, groups_im = [], []
    for r in range(nb // 8):
        gr = re[8 * r:8 * r + 8] + tr * cr - tq * ci
        gi = im[8 * r:8 * r + 8] + tr * ci + tq * cr
        cr, ci = gr[7:8], gi[7:8]
        groups_re.append(gr)
        groups_im.append(gi)
    re, im = jnp.concatenate(groups_re, axis=0), jnp.concatenate(groups_im, axis=0)
    c_re_scr[0:1, :] = cr
    c_im_scr[0:1, :] = ci
    x_re = jnp.where(first, c0r, pltpu.roll(re, 1, 0)).astype(BF16)
    x_im = jnp.where(first, c0i, pltpu.roll(im, 1, 0)).astype(BF16)
    yield
    y = []
    for q in range(n_pairs):
        x_in = jnp.concatenate([x_re[:, 128 * q:128 * (q + 1)], x_im[:, 128 * q:128 * (q + 1)]], axis=1)
        y.append(jnp.dot(ucat[q], tmm[q], preferred_element_type=F32)
                 + jnp.dot(x_in, pmm[q], preferred_element_type=F32))
    yield
    for i in range(sb):
        halves = []
        for t in range(2):
            groups = range(8 * t, 8 * t + 8)
            halves.append(_merge_pieces([y[g // 2][:, 128 * (g % 2):128 * (g % 2 + 1)] for g in groups],
                                        [i] * 8, [g % 8 for g in groups]))
        yi = jnp.concatenate(halves, axis=1)
        yi = jax.nn.gelu(yi + d_ref[...] * jnp.concatenate([ua[i], ub[i]], axis=1))
        out = yi * jax.nn.sigmoid(_dot(yi, gw_ref[...]) + gb_ref[...])
        oa_ref[pl.ds(i, nb, stride=sb), :] = out[:, 0:128]
        ob_ref[pl.ds(i, nb, stride=sb), :] = out[:, 128:256]
        if i % 2 == 1:
            yield


def _s5(za, zb, bst, tm, pm, tab_re, tab_im, d, gw, gb):
    B, Lp, _ = za.shape
    rows = _rows_per_step(B, S5_ROWS)
    nb = _pick_tile(Lp // S5_BLOCK, (264, 24, 16, 8))
    tk = nb * S5_BLOCK
    tok = pl.BlockSpec((rows, tk, 128), lambda b, j: (b, j, 0))
    small = (tab_re, tab_im, d, gw, gb)
    half = jax.ShapeDtypeStruct((B, Lp, 128), F32)
    return pl.pallas_call(
        _interleaved([(functools.partial(_s5_kernel, nb=nb), 2, 10, 2, 2)], rows),
        grid=(B // rows, Lp // tk),
        in_specs=[tok, tok] + [_const_spec(a) for a in (bst, tm, pm) + small],
        out_specs=[tok, tok],
        out_shape=[half, half],
        scratch_shapes=[pltpu.VMEM((rows, 8, S5_LANES), F32), pltpu.VMEM((rows, 8, S5_LANES), F32)],
        compiler_params=pltpu.CompilerParams(dimension_semantics=("parallel", "arbitrary"),
                                             vmem_limit_bytes=VMEM_LIMIT),
        name="s5",
    )(za, zb, bst, tm, pm, *small)


def _rope_tables(lp):
    half = HEAD_DIM // 2
    pos = jnp.arange(lp) - PAD_FRONT
    inv_freq = ROPE_BASE ** (-jnp.arange(half, dtype=F32) / half)
    ang = pos.astype(F32)[:, None] * inv_freq[None, :]
    cos, sin = jnp.cos(ang), jnp.sin(ang)
    cos_t = jnp.tile(jnp.concatenate([cos, cos], axis=1), (1, N_HEADS))
    sin_t = jnp.tile(jnp.concatenate([-sin, sin], axis=1), (1, N_HEADS))
    return cos_t, sin_t


def _retention_tables():
    C = CHUNK
    log_g = jnp.log1p(-jnp.exp2(-5.0 - jnp.arange(N_HEADS, dtype=F32)))
    t = jnp.arange(C, dtype=F32)
    rel = t[:, None] - t[None, :]
    causal = rel >= 0
    dm = jnp.where(causal[None], jnp.exp(jnp.where(causal, rel, 0.0)[None] * log_g[:, None, None]), 0.0)
    lane_g = jnp.repeat(log_g, HEAD_DIM)
    qdec = jnp.exp((t + 1.0)[:, None] * lane_g[None, :])
    kdec_t = jnp.exp(lane_g[:, None] * (C - 1.0 - t)[None, :])
    gcm = jnp.broadcast_to(jnp.exp(C * lane_g)[:, None], (GROUP_W, GROUP_W))
    dm = jnp.transpose(dm, (1, 0, 2)).reshape(C, N_HEADS * C)
    return dm, qdec, kdec_t, gcm


def _s5_tables(a_re, a_im, log_dt, b_re, b_im, c_re, c_im):
    G, sb = S5_GROUPS, S5_BLOCK
    hi = lax.Precision.HIGHEST
    lam_re, lam_im = a_re.astype(F32), a_im.astype(F32)
    dt = jnp.exp(log_dt.astype(F32))[:, None]
    mag, ph = jnp.exp(lam_re * dt), lam_im * dt
    ab_re, ab_im = mag * jnp.cos(ph), mag * jnp.sin(ph)
    den = lam_re * lam_re + lam_im * lam_im
    nr, ni = ab_re - 1.0, ab_im
    zc_re = (nr * lam_re + ni * lam_im) / den
    zc_im = (ni * lam_re - nr * lam_im) / den
    b_re, b_im = b_re.astype(F32), b_im.astype(F32)
    bb_re = zc_re[..., None] * b_re - zc_im[..., None] * b_im
    bb_im = zc_re[..., None] * b_im + zc_im[..., None] * b_re
    c_re, c_im = c_re.astype(F32), c_im.astype(F32)
    eye = jnp.eye(G, dtype=F32)
    n = jnp.arange(sb + 1, dtype=F32)[:, None, None]
    mag_n, ph_n = jnp.exp(lam_re * dt * n), lam_im * dt * n
    pr, pi = mag_n * jnp.cos(ph_n), mag_n * jnp.sin(ph_n)

    gw = sb * S5_CH

    qr, qi = pr[sb - 1::-1, :, :, None], pi[sb - 1::-1, :, :, None]
    w = jnp.stack([qr * bb_re - qi * bb_im, qr * bb_im + qi * bb_re], axis=0)
    bst = jnp.transpose(w, (2, 1, 4, 0, 3)).reshape(G, gw, 2 * S5_STATE)

    m_re = c_re * pr[:, :, None, :] - c_im * pi[:, :, None, :]
    m_im = c_re * pi[:, :, None, :] + c_im * pr[:, :, None, :]
    coef = jnp.stack([m_re[1:], -m_im[1:]], axis=0)
    pm = jnp.transpose(coef, (2, 0, 4, 1, 3)).reshape(G, 2 * S5_STATE, gw)

    kern = (jnp.einsum('ngcp,gpd->ngcd', m_re[:sb], bb_re, precision=hi)
            - jnp.einsum('ngcp,gpd->ngcd', m_im[:sb], bb_im, precision=hi))
    lag = jnp.arange(sb)[None, :] - jnp.arange(sb)[:, None]
    k_ji = jnp.where((lag >= 0)[:, :, None, None, None], kern[jnp.clip(lag, 0, sb - 1)], 0.0)
    tm = jnp.transpose(k_ji, (2, 0, 4, 1, 3)).reshape(G, gw, gw)

    steps = sb * jnp.arange(1, 9, dtype=F32)[:, None, None]
    m, p = jnp.exp(lam_re * dt * steps), lam_im * dt * steps
    tab_re, tab_im = (m * jnp.cos(p)).reshape(8, S5_LANES), (m * jnp.sin(p)).reshape(8, S5_LANES)
    return bst.astype(BF16), tm.astype(BF16), pm.astype(BF16), tab_re, tab_im


def _pick_tile(lp, candidates):
    for t in candidates:
        if lp % t == 0:
            return t
    return CHUNK


def kernel(x, meta_tokens, norm_mix_g, w_in, hgrn_lb_logits, hgrn_norm_g, rwkv_mu, rwkv_w0, rwkv_w_up,
           rwkv_a0, rwkv_a_up, rwkv_g_up, rwkv_k_k, rwkv_k_a, rwkv_r_k, rwkv_ln_w, rwkv_ln_b, rwkv_v0,
           rwkv_v_down, rwkv_v_up, s5_a_re, s5_a_im, s5_log_dt, s5_b_re, s5_b_im, s5_c_re, s5_c_im, s5_d,
           s5_glu_w, s5_glu_b, w_out, norm_ffn_g, w_ffn_up, w_ffn_down, norm_f_g):
    B, seq, _ = x.shape
    depth = w_in.shape[0]
    assert seq % CHUNK == 0
    lp = CHUNK + seq
    tm = _pick_tile(lp, (704, 384, 256, 128))
    tm_last = _pick_tile(seq, (512, 256, 128))
    row2 = lambda a: a.astype(F32).reshape(1, -1)

    meta = jnp.broadcast_to(meta_tokens.astype(x.dtype)[None], (B, N_META, D_MODEL))
    head = jnp.concatenate([jnp.zeros((B, PAD_FRONT, D_MODEL), x.dtype), meta], axis=1)
    if depth > 1:
        stream = (head, x)
    else:
        stream = (jnp.concatenate([head, x], axis=1),)

    p = jax.nn.softmax(hgrn_lb_logits.astype(F32), axis=0)
    lower_bounds = jnp.cumsum(p, axis=0) - p[0]
    cos_t, sin_t = _rope_tables(lp)
    dm, qdec, kdec_t, gcm = _retention_tables()

    def pad_rows(a, n):
        return jnp.pad(a, ((0, n - a.shape[0]), (0, 0)))

    w_in_b = w_in.astype(BF16)
    v_first = None
    for l in range(depth):
        z_ret, z_hgrn, z_rwkv, z_s5a, z_s5b = _in_proj(stream, lp, row2(norm_mix_g[l]), w_in_b[l], tm)

        mu = jnp.pad(rwkv_mu[l].astype(F32), (0, RWKV_COLS_PAD - RWKV_COLS)).reshape(1, -1)
        w_up = pad_rows(rwkv_w_up[l], 128).astype(BF16)
        a_up = jnp.concatenate([jnp.zeros_like(rwkv_a_up[l]), rwkv_a_up[l]], axis=0).astype(BF16)
        g_up = pad_rows(rwkv_g_up[l], GROUP_W).astype(BF16)
        params = (mu, row2(rwkv_w0[l]), w_up, row2(rwkv_a0[l]), a_up, g_up, row2(rwkv_k_k[l]),
                  row2(rwkv_k_a[l]), row2(rwkv_r_k[l]), row2(rwkv_ln_w[l]), row2(rwkv_ln_b[l]))
        vmix = None
        if l > 0:
            v_down = jnp.pad(rwkv_v_down[l - 1], ((0, 0), (0, 128 - rwkv_v_down.shape[-1]))).astype(BF16)
            v_up = pad_rows(rwkv_v_up[l - 1], 128).astype(BF16)
            vmix = (row2(rwkv_v0[l - 1]), v_down, v_up)
        o_ret = _retention(z_ret, cos_t, sin_t, dm, qdec, kdec_t, gcm)
        o_hgrn = _hgrn2(z_hgrn, row2(lower_bounds[l]), row2(jnp.tile(hgrn_norm_g[l], N_HEADS)))
        o_rwkv, v_first = _rwkv7(z_rwkv, v_first, params, vmix)

        tabs = _s5_tables(s5_a_re[l], s5_a_im[l], s5_log_dt[l], s5_b_re[l], s5_b_im[l], s5_c_re[l], s5_c_im[l])
        o_s5a, o_s5b = _s5(z_s5a, z_s5b, *tabs, row2(s5_d[l]), s5_glu_w[l].astype(BF16), row2(s5_glu_b[l]))

        last = l == depth - 1
        h = _out_ffn(stream, lp, (o_ret, o_hgrn, o_rwkv, o_s5a, o_s5b), w_out[l].astype(BF16),
                     row2(norm_ffn_g[l]), w_ffn_up[l].astype(BF16), w_ffn_down[l].astype(BF16),
                     row2(norm_f_g), tm_last if last else tm, final=last)
        stream = (h,)
    return h
```

```python
import functools

import jax
import jax.numpy as jnp
from jax import lax
from jax.experimental import pallas as pl
from jax.experimental.pallas import tpu as pltpu

F32 = jnp.float32
BF16 = jnp.bfloat16

D_MODEL = 1024
N_META = 16
HEAD_DIM = 64
GROUP_W = 256
N_HEADS = 4
D_FF = 4096
ROPE_BASE = 10000.0
NORM_EPS = 1e-6
RWKV_LN_EPS = 64e-5
S5_GROUPS = 16
S5_CH = 16
S5_STATE = 64
S5_LANES = S5_GROUPS * S5_STATE
RWKV_COLS = 1056
RWKV_COLS_PAD = 1152
N_IN = 1024 + 1024 + RWKV_COLS + 256

CHUNK = 128
PAD_FRONT = CHUNK - N_META
VMEM_LIMIT = 56 * 1024 * 1024
RET_ROWS, GLA_ROWS, RWKV_ROWS, S5_ROWS = 8, 8, 8, 2
S5_BLOCK = 8


def _rows_per_step(batch, rows):
    while batch % rows:
        rows -= 1
    return rows


def _dot(a, b):
    return jnp.dot(a.astype(BF16), b.astype(BF16), preferred_element_type=F32)


def _cumsum_rows(x):
    n = x.shape[0]
    tri = (_iota((n, n), 0) >= _iota((n, n), 1)).astype(BF16)
    out, rest = None, x
    for _ in range(3):
        part = rest.astype(BF16)
        rest = rest - part.astype(F32)
        d = jnp.dot(tri, part, preferred_element_type=F32)
        out = d if out is None else out + d
    return out


def _iota(shape, dim):
    return lax.broadcasted_iota(jnp.int32, shape, dim)


def _block_diag_mask():
    return (_iota((GROUP_W, GROUP_W), 0) >> 6) == (_iota((GROUP_W, GROUP_W), 1) >> 6)


def _head_sum(x, bd_ones):
    return jnp.dot(x.astype(BF16), bd_ones, preferred_element_type=F32)


def _silu(x):
    return x * jax.nn.sigmoid(x)


def _interleaved(parts, rows):
    def kern(*refs):
        n_in, n_out = sum(p[2] for p in parts), sum(p[3] for p in parts)
        ins, outs, scrs = refs[:n_in], refs[n_in:n_in + n_out], refs[n_in + n_out:]

        @pl.when(pl.program_id(1) == 0)
        def _():
            for s in scrs:
                s[...] = jnp.zeros_like(s)

        live = []
        for bi in range(rows):
            i0 = o0 = s0 = 0
            for body, n_row_in, n_i, n_o, n_s in parts:
                p_in, p_out, p_scr = ins[i0:i0 + n_i], outs[o0:o0 + n_o], scrs[s0:s0 + n_s]
                i0, o0, s0 = i0 + n_i, o0 + n_o, s0 + n_s
                live.append(body(*[r.at[bi] for r in p_in[:n_row_in]], *p_in[n_row_in:],
                                 *[r.at[bi] for r in p_out], *[s.at[bi] for s in p_scr]))
        while live:
            for g in list(live):
                if next(g, "done") == "done":
                    live.remove(g)
    return kern


def _const_spec(a):
    return pl.BlockSpec(a.shape, lambda b, j: (0,) * a.ndim)


def _elem_spec(tm, n, offset):
    return pl.BlockSpec((pl.Element(1), pl.Element(tm), pl.Element(n)),
                        lambda b, j: (b, pl.multiple_of(offset(j), 128), 0))


def _stream_specs(first, tm):
    if not first:
        return [pl.BlockSpec((None, tm, D_MODEL), lambda b, j: (b, j, 0))]
    return [pl.BlockSpec((None, CHUNK, D_MODEL), lambda b, j: (b, 0, 0)),
            _elem_spec(tm, D_MODEL, lambda j: jnp.maximum(j * tm - CHUNK, 0))]


def _read_stream(first, refs):
    if not first:
        return refs[0][...], refs[1:]
    x = refs[1][0]
    tile0 = jnp.concatenate([refs[0][...], x[:x.shape[0] - CHUNK]], axis=0)
    return jnp.where(pl.program_id(1) == 0, tile0, x), refs[2:]


def _resident(shape):
    return pl.BlockSpec(shape, lambda b, j: (0,) * len(shape), pipeline_mode=pl.Buffered(1))


def _in_proj_kernel(*refs, tm, first):
    x, (g_ref, w_ref, zr_ref, zh_ref, zw_ref, zsa_ref, zsb_ref) = _read_stream(first, refs)
    ms = jnp.mean(x * x, axis=-1, keepdims=True)
    xn = x * lax.rsqrt(ms + NORM_EPS) * g_ref[...]
    row = pl.program_id(1) * tm + _iota((tm, 1), 0)
    xb = jnp.where(row < PAD_FRONT, 0.0, xn).astype(BF16)
    zr_ref[...] = jnp.dot(xb, w_ref[:, 0:1024], preferred_element_type=F32)
    zh_ref[...] = jnp.dot(xb, w_ref[:, 1024:2048], preferred_element_type=F32)
    zw_ref[...] = jnp.dot(xb, w_ref[:, 2048:2048 + RWKV_COLS_PAD], preferred_element_type=F32)
    zs = jnp.dot(xb, w_ref[:, 2048 + RWKV_COLS:N_IN], preferred_element_type=F32)
    zsa_ref[...] = zs[:, 0:128]
    zsb_ref[...] = zs[:, 128:256]


def _in_proj(stream, lp, g, w, tm):
    first = len(stream) == 2
    B = stream[0].shape[0]
    row_spec = lambda n: pl.BlockSpec((None, tm, n), lambda b, j: (b, j, 0))
    widths = (1024, 1024, RWKV_COLS_PAD, 128, 128)
    return pl.pallas_call(
        functools.partial(_in_proj_kernel, tm=tm, first=first),
        grid=(B, lp // tm),
        in_specs=_stream_specs(first, tm) + [_resident((1, D_MODEL)), _resident((D_MODEL, N_IN))],
        out_specs=[row_spec(n) for n in widths],
        out_shape=[jax.ShapeDtypeStruct((B, lp, n), F32) for n in widths],
        compiler_params=pltpu.CompilerParams(dimension_semantics=("parallel", "parallel"),
                                             vmem_limit_bytes=VMEM_LIMIT),
        name="in_proj",
    )(*stream, g, w)


def _ffn_kernel(*refs, final, first, fc):
    if final:
        h, refs = refs[0][0], refs[1:]
        mix = [r[0] for r in refs[:5]]
    else:
        h, refs = _read_stream(first, refs)
        mix = [r[...] for r in refs[:5]]
    wo_ref, g_ref, wu_ref, wd_ref, gf_ref, out_ref = refs[5:]
    o = jnp.concatenate(mix, axis=1).astype(BF16)
    acc = h + jnp.dot(o, wo_ref[...], preferred_element_type=F32)
    ms = jnp.mean(acc * acc, axis=-1, keepdims=True)
    hb = (acc * lax.rsqrt(ms + NORM_EPS) * g_ref[...]).astype(BF16)
    mlp = None
    for c in range(0, D_FF, fc):
        u = jnp.dot(hb, wu_ref[:, c:c + fc], preferred_element_type=F32)
        u = jnp.square(jnp.maximum(u, 0.0)).astype(BF16)
        d = jnp.dot(u, wd_ref[c:c + fc, :], preferred_element_type=F32)
        mlp = d if mlp is None else mlp + d
    y = acc + mlp
    if final:
        ms = jnp.mean(y * y, axis=-1, keepdims=True)
        y = y * lax.rsqrt(ms + NORM_EPS) * gf_ref[...]
    out_ref[...] = y


def _out_ffn(stream, lp, outs, wo, g, wu, wd, gf, tm, final):
    first = len(stream) == 2
    B = stream[0].shape[0]
    row_spec = lambda n: pl.BlockSpec((None, tm, n), lambda b, j: (b, j, 0))
    if final:
        rows_out = lp - CHUNK
        specs = [_elem_spec(tm, a.shape[-1], lambda j: CHUNK + j * tm) for a in (stream[0],) + tuple(outs)]
    else:
        rows_out = lp
        specs = _stream_specs(first, tm) + [row_spec(o.shape[-1]) for o in outs]
    return pl.pallas_call(
        functools.partial(_ffn_kernel, final=final, first=first, fc=1024),
        grid=(B, rows_out // tm),
        in_specs=specs + [_resident((D_MODEL, D_MODEL)), _resident((1, D_MODEL)), _resident((D_MODEL, D_FF)),
                          _resident((D_FF, D_MODEL)), _resident((1, D_MODEL))],
        out_specs=row_spec(D_MODEL),
        out_shape=jax.ShapeDtypeStruct((B, rows_out, D_MODEL), F32),
        compiler_params=pltpu.CompilerParams(dimension_semantics=("parallel", "parallel"),
                                             vmem_limit_bytes=VMEM_LIMIT),
        name="out_ffn",
    )(*stream, *outs, wo, g, wu, wd, gf)


def _ret_kernel(z_ref, cos_ref, sin_ref, dm_ref, qdec_ref, kdec_t_ref, gcm_ref, o_ref, s_scr):
    bdm = _block_diag_mask()
    bd_ones = bdm.astype(BF16)
    z = z_ref[...]
    q, k, v, g = z[:, 0:256], z[:, 256:512], z[:, 512:768], z[:, 768:1024]
    cos, sin = cos_ref[...], sin_ref[...]
    upper = (_iota((1, GROUP_W), 1) & (HEAD_DIM - 1)) >= HEAD_DIM // 2

    def rot(t):
        swapped = jnp.where(upper, pltpu.roll(t, HEAD_DIM // 2, 1),
                            pltpu.roll(t, GROUP_W - HEAD_DIM // 2, 1))
        return t * cos + swapped * sin

    qr = rot(q)
    kr = rot(k) * HEAD_DIM ** -0.5
    yield
    s = s_scr[...]
    o = _dot(qr * qdec_ref[...], s)
    vb = v.astype(BF16)
    kr_t = kr.T
    head_row = [(_iota((GROUP_W, 1), 0) >> 6) == h for h in range(N_HEADS)]
    head_lane = [(_iota((1, GROUP_W), 1) >> 6) == h for h in range(N_HEADS)]
    kr_tb = kr_t.astype(BF16)
    sc = jnp.dot(qr.astype(BF16), jnp.concatenate([jnp.where(hr, kr_tb, 0.0) for hr in head_row], axis=1),
                 preferred_element_type=F32)
    kv = _dot(kr_t * kdec_t_ref[...], vb)
    s_scr[...] = gcm_ref[...] * s + jnp.where(bdm, kv, 0.0)
    yield
    v_heads = jnp.concatenate([jnp.where(hl, vb, 0.0) for hl in head_lane], axis=0)
    o = o + jnp.dot((sc * dm_ref[...]).astype(BF16), v_heads, preferred_element_type=F32)
    yield
    msq = _head_sum(o * o, bd_ones) * (1.0 / HEAD_DIM)
    o_ref[...] = o * lax.rsqrt(msq + NORM_EPS) * _silu(g)


def _gla_kernel(z_ref, lb_ref, ng_ref, o_ref, s_scr):
    C = CHUNK
    bdm = _block_diag_mask()
    bd_ones = bdm.astype(BF16)
    z = z_ref[...]
    q, f, v, g =z[:, 0:256], z[:, 256:512], z[:, 512:768], z[:, 768:1024]
    lb = lb_ref[...]
    forget = lb + (1.0 - lb) * jax.nn.sigmoid(f)
    k = 1.0 - forget
    logf = jnp.log(forget)
    q = _silu(q) * HEAD_DIM ** -0.5
    row = _iota((C, 1), 0)
    ti, si = _iota((C, C), 0), _iota((C, C), 1)
    b = _cumsum_rows(logf)
    yield

    o = jnp.dot((q * k).astype(BF16), bd_ones, preferred_element_type=F32) * v

    head_lane = [(_iota((1, GROUP_W), 1) >> 6) == h for h in range(N_HEADS)]
    head_row = [(_iota((GROUP_W, 1), 0) >> 6) == h for h in range(N_HEADS)]
    b8 = b.reshape(C // 8, 8, GROUP_W)
    p = None
    m = 1
    while 2 * m <= C:
        if m == 1:
            ref = jnp.where((row & 1) == 1, pltpu.roll(b, 1, 0), b)
        elif m == 2:
            ref = jnp.where(_iota((1, 8, 1), 1) < 4, b8[:, 1:2, :], b8[:, 5:6, :]).reshape(C, GROUP_W)
        else:
            b3 = b.reshape(C // (2 * m), 2 * m, GROUP_W)
            ref = jnp.broadcast_to(b3[:, m - 1:m, :], b3.shape).reshape(C, GROUP_W)
        second = (row & (2 * m - 1)) >= m
        qm = jnp.where(second, q * jnp.exp(b - ref), 0.0).astype(BF16)
        km_t = jnp.where(second, 0.0, k * jnp.exp(ref - b)).T.astype(BF16)
        rk = jnp.concatenate([jnp.where(hr, km_t, 0.0) for hr in head_row], axis=1)
        sc = jnp.dot(qm, rk, preferred_element_type=F32)
        if 2 * m < C:
            sc = jnp.where(jnp.concatenate([(ti ^ si) < 2 * m] * N_HEADS, axis=1), sc, 0.0)
        p = sc if p is None else p + sc
        m *= 2
        yield
    vb = v.astype(BF16)
    v_heads = jnp.concatenate([jnp.where(hl, vb, 0.0) for hl in head_lane], axis=0)
    o = o + jnp.dot(p.astype(BF16), v_heads, preferred_element_type=F32)
    yield

    s = s_scr[...]
    o = o + _dot(q * jnp.exp(b), s)
    b_t = b.T
    b_last = b_t[:, C - 1:C]
    k_end = k.T * jnp.exp(b_last - b_t)
    s_scr[...] = jnp.exp(b_last) * s + jnp.where(bdm, _dot(k_end, vb), 0.0)
    yield
    msq = _head_sum(o * o, bd_ones) * (1.0 / HEAD_DIM)
    o_ref[...] = o * lax.rsqrt(msq + NORM_EPS) * ng_ref[...] * _silu(g)


def _rwkv_kernel(*refs, has_vmix):
    C = CHUNK
    if has_vmix:
        (z_ref, vf_ref, mu_ref, w0_ref, wup_ref, a0_ref, aup_ref, gup_ref, kk_ref, ka_ref, rk_ref,
         lnw_ref, lnb_ref, v0_ref, vdn_ref, vup_ref, o_ref, t_scr, carry_scr) = refs
    else:
        (z_ref, mu_ref, w0_ref, wup_ref, a0_ref, aup_ref, gup_ref, kk_ref, ka_ref, rk_ref,
         lnw_ref, lnb_ref, o_ref, vf_out_ref, t_scr, carry_scr) = refs

    bdm = _block_diag_mask()
    bd_ones = bdm.astype(BF16)
    row = _iota((C, 1), 0)
    ti, si = _iota((C, C), 0), _iota((C, C), 1)

    z = z_ref[...]
    z_prev = jnp.where(row == 0, carry_scr[0:1, :], pltpu.roll(z, 1, 0))
    carry_scr[0:1, :] = z[C - 1:C, :]
    zf = z + (z_prev - z) * mu_ref[...]
    r, k, v = zf[:, 0:256], zf[:, 256:512], zf[:, 512:768]
    wa, gl = zf[:, 768:896], zf[:, 896:RWKV_COLS_PAD]

    w_log = -jax.nn.softplus(-(w0_ref[...] + _dot(jnp.tanh(wa), wup_ref[...]))) - 0.5
    logw = -jnp.exp(w_log)
    a = jax.nn.sigmoid(a0_ref[...] + _dot(wa, aup_ref[...]))
    gate = _dot(jax.nn.sigmoid(gl), gup_ref[...])
    if has_vmix:
        mix = jax.nn.sigmoid(v0_ref[...] + _dot(_dot(v, vdn_ref[...]), vup_ref[...]))
        v = v + (vf_ref[...] - v) * mix
    else:
        vf_out_ref[...] = v
    kk = k * kk_ref[...]
    kk = kk / jnp.maximum(jnp.sqrt(_head_sum(kk * kk, bd_ones)), 1e-12)
    k = k * (1.0 + (a - 1.0) * ka_ref[...])
    alpha = -kk * a
    yield

    gc = _cumsum_rows(logw)
    yield
    gp = gc - logw
    gm = gc[C // 2:C // 2 + 1, :]
    r_mid, b_mid = r * jnp.exp(gc - gm), kk * jnp.exp(gp - gm)
    r_abs, b_abs = r * jnp.exp(gc), kk * jnp.exp(gp)
    g_t, a_t, k_t = gc.T, alpha.T, k.T
    from_mid = jnp.exp(g_t[:, C // 2:C // 2 + 1] - g_t)
    g_last = g_t[:, C - 1:C]
    to_end = jnp.exp(g_last - g_t)

    yield
    head_rows = [(_iota((GROUP_W, 1), 0) >> 6) == h for h in range(N_HEADS)]
    a_mid, k_mid = a_t * from_mid, k_t * from_mid
    rhs_t = jnp.concatenate([jnp.where(m, a_mid, 0.0) for m in head_rows]
                            + [jnp.where(m, k_mid, 0.0) for m in head_rows], axis=1)
    sc = _dot(jnp.concatenate([b_mid, r_mid], axis=0), rhs_t)
    yield

    t0 = t_scr[...]
    t0b = t0.astype(BF16)
    strict2 = jnp.concatenate([ti > si] * 2, axis=1)
    incl2 = jnp.concatenate([ti >= si] * 2, axis=1)
    pairs = range(N_HEADS // 2)
    head_lane = [(_iota((1, GROUP_W), 1) >> 6) == h for h in range(N_HEADS)]
    zeros_cc = jnp.zeros((C, C), BF16)

    def stack(x, p):
        xb = x.astype(BF16)
        return jnp.concatenate([jnp.where(head_lane[2 * p], xb, 0.0),
                                jnp.where(head_lane[2 * p + 1], xb, 0.0)], axis=0)

    def block_diag(xab):
        xb = xab.astype(BF16)
        return jnp.concatenate([jnp.concatenate([xb[:, 0:C], zeros_cc], axis=1),
                                jnp.concatenate([zeros_cc, xb[:, C:2 * C]], axis=1)], axis=0)

    v_stack = [stack(v, p) for p in pairs]
    rhs = jnp.dot(b_abs.astype(BF16), t0b, preferred_element_type=F32)
    for p in pairs:
        bm = jnp.where(strict2, sc[0:C, (N_HEADS + 2 * p) * C:(N_HEADS + 2 * p + 2) * C], 0.0)
        rhs = rhs + jnp.dot(bm.astype(BF16), v_stack[p], preferred_element_type=F32)
    yield

    a_ab = [jnp.where(strict2, sc[0:C, 2 * p * C:(2 * p + 2) * C], 0.0) for p in pairs]
    lower_left = lambda m: jnp.concatenate(
        [((ti & (2 * m - 1)) >= m) & ((si & (2 * m - 1)) < m) & ((ti ^ si) < 2 * m)] * 2, axis=1)
    eye2 = jnp.concatenate([ti == si] * 2, axis=1)
    x_ab = [jnp.where(eye2, 1.0, jnp.where(lower_left(1), a_ab[p], 0.0)) for p in pairs]
    for lvl in range(1, C.bit_length() - 1):
        sel = lower_left(1 << lvl)
        for p in pairs:
            y = jnp.dot(jnp.where(sel, a_ab[p], 0.0).astype(BF16), block_diag(x_ab[p]),
                        preferred_element_type=F32)
            x_ab[p] = x_ab[p] + jnp.dot(x_ab[p].astype(BF16), block_diag(y), preferred_element_type=F32)
        yield
    n_ab = [jnp.where(eye2, 0.0, x_ab[p]) for p in pairs]

    u = rhs + sum(jnp.dot(n_ab[p].astype(BF16), stack(rhs, p), preferred_element_type=F32) for p in pairs)
    yield
    o = jnp.dot(r_abs.astype(BF16), t0b, preferred_element_type=F32)
    for p in pairs:
        ao = jnp.where(incl2, sc[C:2 * C, 2 * p * C:(2 * p + 2) * C], 0.0)
        bo = jnp.where(incl2, sc[C:2 * C, (N_HEADS + 2 * p) * C:(N_HEADS + 2 * p + 2) * C], 0.0)
        o = o + jnp.dot(jnp.concatenate([ao, bo], axis=1).astype(BF16),
                        jnp.concatenate([stack(u, p), v_stack[p]], axis=0), preferred_element_type=F32)
    uv = jnp.concatenate([u, v], axis=0).astype(BF16)
    upd = _dot(jnp.concatenate([a_t * to_end, k_t * to_end], axis=1), uv)
    t_scr[...] = jnp.exp(g_last) * t0 + jnp.where(bdm, upd, 0.0)
    yield

    inv_d = 1.0 / HEAD_DIM
    dev = o - _head_sum(o, bd_ones) * inv_d
    var = _head_sum(dev * dev, bd_ones) * inv_d
    on = dev * lax.rsqrt(var + RWKV_LN_EPS) * lnw_ref[...] + lnb_ref[...]
    on = on + _head_sum(r * k * rk_ref[...], bd_ones) * v
    o_ref[...] = on * gate


def _chunk_mixer(body, rows_per_step, row_args, const_specs, const_args, n_out, scratch, name):
    B, Lp, _ = row_args[0].shape
    rows = _rows_per_step(B, rows_per_step)
    tok = lambda n: pl.BlockSpec((rows, CHUNK, n), lambda b, j: (b, j, 0))
    res = pl.pallas_call(
        _interleaved([(body, len(row_args), len(row_args) + len(const_args), n_out, len(scratch))], rows),
        grid=(B // rows, Lp // CHUNK),
        in_specs=[tok(a.shape[-1]) for a in row_args] + list(const_specs),
        out_specs=[tok(GROUP_W)] * n_out,
        out_shape=[jax.ShapeDtypeStruct((B, Lp, GROUP_W), F32)] * n_out,
        scratch_shapes=[pltpu.VMEM((rows,) + s, F32) for s in scratch],
        compiler_params=pltpu.CompilerParams(dimension_semantics=("parallel", "arbitrary")),
        name=name,
    )(*row_args, *const_args)
    return res


def _retention(z, cos, sin, dm, qdec, kdec_t, gcm):
    pos_spec = pl.BlockSpec((CHUNK, GROUP_W), lambda b, j: (j, 0))
    consts = (dm, qdec, kdec_t, gcm)
    return _chunk_mixer(_ret_kernel, RET_ROWS, [z], [pos_spec, pos_spec] + [_const_spec(a) for a in consts],
                        (cos, sin) + consts, 1, [(GROUP_W, GROUP_W)], "retention")[0]


def _hgrn2(z, lb, ng):
    return _chunk_mixer(_gla_kernel, GLA_ROWS, [z], [_const_spec(lb), _const_spec(ng)], (lb, ng), 1,
                        [(GROUP_W, GROUP_W)], "hgrn2")[0]


def _rwkv7(z, v_first, params, vmix):
    has_vmix = vmix is not None
    consts = tuple(params) + (tuple(vmix) if has_vmix else ())
    res = _chunk_mixer(functools.partial(_rwkv_kernel, has_vmix=has_vmix), RWKV_ROWS,
                       [z, v_first] if has_vmix else [z], [_const_spec(a) for a in consts], consts,
                       1 if has_vmix else 2, [(GROUP_W, GROUP_W), (8, RWKV_COLS_PAD)], "rwkv7")
    return (res[0], v_first) if has_vmix else (res[0], res[1])


def _merge_pieces(sources, src_piece, dst_piece):
    piece = _iota((1, 128), 1) >> 4
    out = None
    for x, s, d in zip(sources, src_piece, dst_piece):
        x = x if s == d else pltpu.roll(x, (16 * (d - s)) % 128, 1)
        out = x if out is None else jnp.where(piece == d, x, out)
    return out


def _s5_kernel(ua_ref, ub_ref, bst_ref, tm_ref, pm_ref, tab_re_ref, tab_im_ref, d_ref, gw_ref, gb_ref,
               oa_ref, ob_ref, c_re_scr, c_im_scr, *, nb):
    sb, n_pairs = S5_BLOCK, S5_GROUPS // 2
    ua = [ua_ref[pl.ds(i, nb, stride=sb), :] for i in range(sb)]
    ub = [ub_ref[pl.ds(i, nb, stride=sb), :] for i in range(sb)]
    tokens = list(range(sb))
    ucat = []
    for q in range(n_pairs):
        src = ua if q < n_pairs // 2 else ub
        ucat.append(jnp.concatenate(
            [_merge_pieces(src, [(2 * q + gg) % 8] * sb, tokens) for gg in range(2)], axis=1).astype(BF16))
    yield
    z64, z128, half = jnp.zeros((128, 64), BF16), jnp.zeros((128, 128), BF16), S5_STATE
    bst, tmm, pmm = [], [], []
    for q in range(n_pairs):
        b0, b1 = bst_ref[2 * q], bst_ref[2 * q + 1]
        bst.append(jnp.concatenate([
            jnp.concatenate([b0[:, 0:half], z64, b0[:, half:], z64], axis=1),
            jnp.concatenate([z64, b1[:, 0:half], z64, b1[:, half:]], axis=1)], axis=0))
        tmm.append(jnp.concatenate([jnp.concatenate([tm_ref[2 * q], z128], axis=1),
                                    jnp.concatenate([z128, tm_ref[2 * q + 1]], axis=1)], axis=0))
        p0, p1 = pm_ref[2 * q], pm_ref[2 * q + 1]
        pmm.append(jnp.concatenate([
            jnp.concatenate([p0[0:half], z128[0:half]], axis=1), jnp.concatenate([z128[0:half], p1[0:half]], axis=1),
            jnp.concatenate([p0[half:], z128[0:half]], axis=1), jnp.concatenate([z128[0:half], p1[half:]], axis=1)],
            axis=0))
    w = [jnp.dot(ucat[q], bst[q], preferred_element_type=F32) for q in range(n_pairs)]
    re = jnp.concatenate([x[:, 0:128] for x in w], axis=1)
    im = jnp.concatenate([x[:, 128:256] for x in w], axis=1)
    yield
    row = _iota((nb, 1), 0)
    first = row == 0
    for sh in (1, 2, 4):
        ar, ai = tab_re_ref[sh - 1:sh, :], tab_im_ref[sh - 1:sh, :]
        inside = (row & 7) >= sh
        sre = jnp.where(inside, pltpu.roll(re, sh, 0), 0.0)
        sim = jnp.where(inside, pltpu.roll(im, sh, 0), 0.0)
        re, im = re + ar * sre - ai * sim, im + ar * sim + ai * sre
    yield
    c0r, c0i = c_re_scr[0:1, :], c_im_scr[0:1, :]
    cr, ci = c0r, c0i
    tr, tq = tab_re_ref[...], tab_im_ref[...]
    groups_re, groups_im = [], []
    for r in range(nb // 8):
        gr = re[8 * r:8 * r + 8] + tr * cr - tq * ci
        gi = im[8 * r:8 * r + 8] + tr * ci + tq * cr
        cr, ci = gr[7:8], gi[7:8]
        groups_re.append(gr)
        groups_im.append(gi)
    re, im = jnp.concatenate(groups_re, axis=0), jnp.concatenate(groups_im, axis=0)
    c_re_scr[0:1, :] = cr
    c_im_scr[0:1, :] = ci
    x_re = jnp.where(first, c0r, pltpu.roll(re, 1, 0)).astype(BF16)
    x_im = jnp.where(first, c0i, pltpu.roll(im, 1, 0)).astype(BF16)
    yield
    y = []
    for q in range(n_pairs):
        x_in = jnp.concatenate([x_re[:, 128 * q:128 * (q + 1)], x_im[:, 128 * q:128 * (q + 1)]], axis=1)
        y.append(jnp.dot(ucat[q], tmm[q], preferred_element_type=F32)
                 + jnp.dot(x_in, pmm[q], preferred_element_type=F32))
    yield
    for i in range(sb):
        halves = []
        for t in range(2):
            groups = range(8 * t, 8 * t + 8)
            halves.append(_merge_pieces([y[g // 2][:, 128 * (g % 2):128 * (g % 2 + 1)] for g in groups],
                                        [i] * 8, [g % 8 for g in groups]))
        yi = jnp.concatenate(halves, axis=1)
        yi = jax.nn.gelu(yi + d_ref[...] * jnp.concatenate([ua[i], ub[i]], axis=1))
        out = yi * jax.nn.sigmoid(_dot(yi, gw_ref[...]) + gb_ref[...])
        oa_ref[pl.ds(i, nb, stride=sb), :] = out[:, 0:128]
        ob_ref[pl.ds(i, nb, stride=sb), :] = out[:, 128:256]
        if i % 2 == 1:
            yield


def _s5(za, zb, bst, tm, pm, tab_re, tab_im, d, gw, gb):
    B, Lp, _ = za.shape
    rows = _rows_per_step(B, S5_ROWS)
    nb = _pick_tile(Lp // S5_BLOCK, (264, 24, 16, 8))
    tk = nb * S5_BLOCK
    tok = pl.BlockSpec((rows, tk, 128), lambda b, j: (b, j, 0))
    small = (tab_re, tab_im, d, gw, gb)
    half = jax.ShapeDtypeStruct((B, Lp, 128), F32)
    return pl.pallas_call(
        _interleaved([(functools.partial(_s5_kernel, nb=nb), 2, 10, 2, 2)], rows),
        grid=(B // rows, Lp // tk),
        in_specs=[tok, tok] + [_const_spec(a) for a in (bst, tm, pm) + small],
        out_specs=[tok, tok],
        out_shape=[half, half],
        scratch_shapes=[pltpu.VMEM((rows, 8, S5_LANES), F32), pltpu.VMEM((rows, 8, S5_LANES), F32)],
        compiler_params=pltpu.CompilerParams(dimension_semantics=("parallel", "arbitrary"),
                                             vmem_limit_bytes=VMEM_LIMIT),
        name="s5",
    )(za, zb, bst, tm, pm, *small)


def _rope_tables(lp):
    half = HEAD_DIM // 2
    pos = jnp.arange(lp) - PAD_FRONT
    inv_freq = ROPE_BASE ** (-jnp.arange(half, dtype=F32) / half)
    ang = pos.astype(F32)[:, None] * inv_freq[None, :]
    cos, sin = jnp.cos(ang), jnp.sin(ang)
    cos_t = jnp.tile(jnp.concatenate([cos, cos], axis=1), (1, N_HEADS))
    sin_t = jnp.tile(jnp.concatenate([-sin, sin], axis=1), (1, N_HEADS))
    return cos_t, sin_t


def _retention_tables():
    C = CHUNK
    log_g = jnp.log1p(-jnp.exp2(-5.0 - jnp.arange(N_HEADS, dtype=F32)))
    t = jnp.arange(C, dtype=F32)
    rel = t[:, None] - t[None, :]
    causal = rel >= 0
    dm = jnp.where(causal[None], jnp.exp(jnp.where(causal, rel, 0.0)[None] * log_g[:, None, None]), 0.0)
    lane_g = jnp.repeat(log_g, HEAD_DIM)
    qdec = jnp.exp((t + 1.0)[:, None] * lane_g[None, :])
    kdec_t = jnp.exp(lane_g[:, None] * (C - 1.0 - t)[None, :])
    gcm = jnp.broadcast_to(jnp.exp(C * lane_g)[:, None], (GROUP_W, GROUP_W))
    dm = jnp.transpose(dm, (1, 0, 2)).reshape(C, N_HEADS * C)
    return dm, qdec, kdec_t, gcm


def _s5_tables(a_re, a_im, log_dt, b_re, b_im, c_re, c_im):
    G, sb = S5_GROUPS, S5_BLOCK
    hi = lax.Precision.HIGHEST
    lam_re, lam_im = a_re.astype(F32), a_im.astype(F32)
    dt = jnp.exp(log_dt.astype(F32))[:, None]
    mag, ph = jnp.exp(lam_re * dt), lam_im * dt
    ab_re, ab_im = mag * jnp.cos(ph), mag * jnp.sin(ph)
    den = lam_re * lam_re + lam_im * lam_im
    nr, ni = ab_re - 1.0, ab_im
    zc_re = (nr * lam_re + ni * lam_im) / den
    zc_im = (ni * lam_re - nr * lam_im) / den
    b_re, b_im = b_re.astype(F32), b_im.astype(F32)
    bb_re = zc_re[..., None] * b_re - zc_im[..., None] * b_im
    bb_im = zc_re[..., None] * b_im + zc_im[..., None] * b_re
    c_re, c_im = c_re.astype(F32), c_im.astype(F32)
    eye = jnp.eye(G, dtype=F32)
    n = jnp.arange(sb + 1, dtype=F32)[:, None, None]
    mag_n, ph_n = jnp.exp(lam_re * dt * n), lam_im * dt * n
    pr, pi = mag_n * jnp.cos(ph_n), mag_n * jnp.sin(ph_n)

    gw = sb * S5_CH

    qr, qi = pr[sb - 1::-1, :, :, None], pi[sb - 1::-1, :, :, None]
    w = jnp.stack([qr * bb_re - qi * bb_im, qr * bb_im + qi * bb_re], axis=0)
    bst = jnp.transpose(w, (2, 1, 4, 0, 3)).reshape(G, gw, 2 * S5_STATE)

    m_re = c_re * pr[:, :, None, :] - c_im * pi[:, :, None, :]
    m_im = c_re * pi[:, :, None, :] + c_im * pr[:, :, None, :]
    coef = jnp.stack([m_re[1:], -m_im[1:]], axis=0)
    pm = jnp.transpose(coef, (2, 0, 4, 1, 3)).reshape(G, 2 * S5_STATE, gw)

    kern = (jnp.einsum('ngcp,gpd->ngcd', m_re[:sb], bb_re, precision=hi)
            - jnp.einsum('ngcp,gpd->ngcd', m_im[:sb], bb_im, precision=hi))
    lag = jnp.arange(sb)[None, :] - jnp.arange(sb)[:, None]
    k_ji = jnp.where((lag >= 0)[:, :, None, None, None], kern[jnp.clip(lag, 0, sb - 1)], 0.0)
    tm = jnp.transpose(k_ji, (2, 0, 4, 1, 3)).reshape(G, gw, gw)

    steps = sb * jnp.arange(1, 9, dtype=F32)[:, None, None]
    m, p = jnp.exp(lam_re * dt * steps), lam_im * dt * steps
    tab_re, tab_im = (m * jnp.cos(p)).reshape(8, S5_LANES), (m * jnp.sin(p)).reshape(8, S5_LANES)
    return bst.astype(BF16), tm.astype(BF16), pm.astype(BF16), tab_re, tab_im


def _pick_tile(lp, candidates):
    for t in candidates:
        if lp % t == 0:
            return t
    return CHUNK


def kernel(x, meta_tokens, norm_mix_g, w_in, hgrn_lb_logits, hgrn_norm_g, rwkv_mu, rwkv_w0, rwkv_w_up,
           rwkv_a0, rwkv_a_up, rwkv_g_up, rwkv_k_k, rwkv_k_a, rwkv_r_k, rwkv_ln_w, rwkv_ln_b, rwkv_v0,
           rwkv_v_down, rwkv_v_up, s5_a_re, s5_a_im, s5_log_dt, s5_b_re, s5_b_im, s5_c_re, s5_c_im, s5_d,
           s5_glu_w, s5_glu_b, w_out, norm_ffn_g, w_ffn_up, w_ffn_down, norm_f_g):
    B, seq, _ = x.shape
    depth = w_in.shape[0]
    assert seq % CHUNK == 0
    lp = CHUNK + seq
    tm = _pick_tile(lp, (704, 384, 256, 128))
    tm_last = _pick_tile(seq, (1024, 512, 256, 128))
    row2 = lambda a: a.astype(F32).reshape(1, -1)

    meta = jnp.broadcast_to(meta_tokens.astype(x.dtype)[None], (B, N_META, D_MODEL))
    head = jnp.concatenate([jnp.zeros((B, PAD_FRONT, D_MODEL), x.dtype), meta], axis=1)
    if depth > 1:
        stream = (head, x)
    else:
        stream = (jnp.concatenate([head, x], axis=1),)

    p = jax.nn.softmax(hgrn_lb_logits.astype(F32), axis=0)
    lower_bounds = jnp.cumsum(p, axis=0) - p[0]
    cos_t, sin_t = _rope_tables(lp)
    dm, qdec, kdec_t, gcm = _retention_tables()

    def pad_rows(a, n):
        return jnp.pad(a, ((0, n - a.shape[0]), (0, 0)))

    w_in_b = w_in.astype(BF16)
    v_first = None
    for l in range(depth):
        z_ret, z_hgrn, z_rwkv, z_s5a, z_s5b = _in_proj(stream, lp, row2(norm_mix_g[l]), w_in_b[l], tm)

        mu = jnp.pad(rwkv_mu[l].astype(F32), (0, RWKV_COLS_PAD - RWKV_COLS)).reshape(1, -1)
        w_up = pad_rows(rwkv_w_up[l], 128).astype(BF16)
        a_up = jnp.concatenate([jnp.zeros_like(rwkv_a_up[l]), rwkv_a_up[l]], axis=0).astype(BF16)
        g_up = pad_rows(rwkv_g_up[l], GROUP_W).astype(BF16)
        params = (mu, row2(rwkv_w0[l]), w_up, row2(rwkv_a0[l]), a_up, g_up, row2(rwkv_k_k[l]),
                  row2(rwkv_k_a[l]), row2(rwkv_r_k[l]), row2(rwkv_ln_w[l]), row2(rwkv_ln_b[l]))
        vmix = None
        if l > 0:
            v_down = jnp.pad(rwkv_v_down[l - 1], ((0, 0), (0, 128 - rwkv_v_down.shape[-1]))).astype(BF16)
            v_up = pad_rows(rwkv_v_up[l - 1], 128).astype(BF16)
            vmix = (row2(rwkv_v0[l - 1]), v_down, v_up)
        o_ret = _retention(z_ret, cos_t, sin_t, dm, qdec, kdec_t, gcm)
        o_hgrn = _hgrn2(z_hgrn, row2(lower_bounds[l]), row2(jnp.tile(hgrn_norm_g[l], N_HEADS)))
        o_rwkv, v_first = _rwkv7(z_rwkv, v_first, params, vmix)

        tabs = _s5_tables(s5_a_re[l], s5_a_im[l], s5_log_dt[l], s5_b_re[l], s5_b_im[l], s5_c_re[l], s5_c_im[l])
        o_s5a, o_s5b = _s5(z_s5a, z_s5b, *tabs, row2(s5_d[l]), s5_glu_w[l].astype(BF16), row2(s5_glu_b[l]))

        last = l == depth - 1
        h = _out_ffn(stream, lp, (o_ret, o_hgrn, o_rwkv, o_s5a, o_s5b), w_out[l].astype(BF16),
                     row2(norm_ffn_g[l]), w_ffn_up[l].astype(BF16), w_ffn_down[l].astype(BF16),
                     row2(norm_f_g), tm_last if last else tm, final=last)
        stream = (h,)
    return h
```

```python
import functools

import jax
import jax.numpy as jnp
from jax import lax
from jax.experimental import pallas as pl
from jax.experimental.pallas import tpu as pltpu

F32 = jnp.float32
BF16 = jnp.bfloat16

D_MODEL = 1024
N_META = 16
HEAD_DIM = 64
GROUP_W = 256
N_HEADS = 4
D_FF = 4096
ROPE_BASE = 10000.0
NORM_EPS = 1e-6
RWKV_LN_EPS = 64e-5
S5_GROUPS = 16
S5_CH = 16
S5_STATE = 64
S5_LANES = S5_GROUPS * S5_STATE
RWKV_COLS = 1056
RWKV_COLS_PAD = 1152
N_IN = 1024 + 1024 + RWKV_COLS + 256

CHUNK = 128
PAD_FRONT = CHUNK - N_META
VMEM_LIMIT = 56 * 1024 * 1024
RET_ROWS, GLA_ROWS, RWKV_ROWS, S5_ROWS = 8, 8, 8, 2
S5_BLOCK = 8


def _rows_per_step(batch, rows):
    while batch % rows:
        rows -= 1
    return rows


def _dot(a, b):
    return jnp.dot(a.astype(BF16), b.astype(BF16), preferred_element_type=F32)


def _cumsum_rows(x):
    n = x.shape[0]
    tri = (_iota((n, n), 0) >= _iota((n, n), 1)).astype(BF16)
    out, rest = None, x
    for _ in range(3):
        part = rest.astype(BF16)
        rest = rest - part.astype(F32)
        d = jnp.dot(tri, part, preferred_element_type=F32)
        out = d if out is None else out + d
    return out


def _iota(shape, dim):
    return lax.broadcasted_iota(jnp.int32, shape, dim)


def _block_diag_mask():
    return (_iota((GROUP_W, GROUP_W), 0) >> 6) == (_iota((GROUP_W, GROUP_W), 1) >> 6)


def _head_sum(x, bd_ones):
    return jnp.dot(x.astype(BF16), bd_ones, preferred_element_type=F32)


def _silu(x):
    return x * jax.nn.sigmoid(x)


def _interleaved(parts, rows):
    def kern(*refs):
        n_in, n_out = sum(p[2] for p in parts), sum(p[3] for p in parts)
        ins, outs, scrs = refs[:n_in], refs[n_in:n_in + n_out], refs[n_in + n_out:]

        @pl.when(pl.program_id(1) == 0)
        def _():
            for s in scrs:
                s[...] = jnp.zeros_like(s)

        live = []
        for bi in range(rows):
            i0 = o0 = s0 = 0
            for body, n_row_in, n_i, n_o, n_s in parts:
                p_in, p_out, p_scr = ins[i0:i0 + n_i], outs[o0:o0 + n_o], scrs[s0:s0 + n_s]
                i0, o0, s0 = i0 + n_i, o0 + n_o, s0 + n_s
                live.append(body(*[r.at[bi] for r in p_in[:n_row_in]], *p_in[n_row_in:],
                                 *[r.at[bi] for r in p_out], *[s.at[bi] for s in p_scr]))
        while live:
            for g in list(live):
                if next(g, "done") == "done":
                    live.remove(g)
    return kern


def _const_spec(a):
    return pl.BlockSpec(a.shape, lambda b, j: (0,) * a.ndim)


def _elem_spec(tm, n, offset):
    return pl.BlockSpec((pl.Element(1), pl.Element(tm), pl.Element(n)),
                        lambda b, j: (b, pl.multiple_of(offset(j), 128), 0))


def _stream_specs(first, tm):
    if not first:
        return [pl.BlockSpec((None, tm, D_MODEL), lambda b, j: (b, j, 0))]
    return [pl.BlockSpec((None, CHUNK, D_MODEL), lambda b, j: (b, 0, 0)),
            _elem_spec(tm, D_MODEL, lambda j: jnp.maximum(j * tm - CHUNK, 0))]


def _read_stream(first, refs):
    if not first:
        return refs[0][...], refs[1:]
    x = refs[1][0]
    tile0 = jnp.concatenate([refs[0][...], x[:x.shape[0] - CHUNK]], axis=0)
    return jnp.where(pl.program_id(1) == 0, tile0, x), refs[2:]


def _resident(shape):
    return pl.BlockSpec(shape, lambda b, j: (0,) * len(shape), pipeline_mode=pl.Buffered(1))


def _in_proj_kernel(*refs, tm, first):
    x, (g_ref, w_ref, zr_ref, zh_ref, zw_ref, zsa_ref, zsb_ref) = _read_stream(first, refs)
    ms = jnp.mean(x * x, axis=-1, keepdims=True)
    xn = x * lax.rsqrt(ms + NORM_EPS) * g_ref[...]
    row = pl.program_id(1) * tm + _iota((tm, 1), 0)
    xb = jnp.where(row < PAD_FRONT, 0.0, xn).astype(BF16)
    zr_ref[...] = jnp.dot(xb, w_ref[:, 0:1024], preferred_element_type=F32)
    zh_ref[...] = jnp.dot(xb, w_ref[:, 1024:2048], preferred_element_type=F32)
    zw_ref[...] = jnp.dot(xb, w_ref[:, 2048:2048 + RWKV_COLS_PAD], preferred_element_type=F32)
    zs = jnp.dot(xb, w_ref[:, 2048 + RWKV_COLS:N_IN], preferred_element_type=F32)
    zsa_ref[...] = zs[:, 0:128]
    zsb_ref[...] = zs[:, 128:256]


def _in_proj(stream, lp, g, w, tm):
    first = len(stream) == 2
    B = stream[0].shape[0]
    row_spec = lambda n: pl.BlockSpec((None, tm, n), lambda b, j: (b, j, 0))
    widths = (1024, 1024, RWKV_COLS_PAD, 128, 128)
    return pl.pallas_call(
        functools.partial(_in_proj_kernel, tm=tm, first=first),
        grid=(B, lp // tm),
        in_specs=_stream_specs(first, tm) + [_resident((1, D_MODEL)), _resident((D_MODEL, N_IN))],
        out_specs=[row_spec(n) for n in widths],
        out_shape=[jax.ShapeDtypeStruct((B, lp, n), F32) for n in widths],
        compiler_params=pltpu.CompilerParams(dimension_semantics=("parallel", "parallel"),
                                             vmem_limit_bytes=VMEM_LIMIT,
                                             allow_input_fusion=[False] * (len(stream) + 1) + [True]),
        name="in_proj",
    )(*stream, g, w)


def _ffn_kernel(*refs, final, first, fc):
    if final:
        h, refs = refs[0][0], refs[1:]
        mix = [r[0] for r in refs[:5]]
    else:
        h, refs = _read_stream(first, refs)
        mix = [r[...] for r in refs[:5]]
    wo_ref, g_ref, wu_ref, wd_ref, gf_ref, out_ref = refs[5:]
    o = jnp.concatenate(mix, axis=1).astype(BF16)
    acc = h + jnp.dot(o, wo_ref[...], preferred_element_type=F32)
    ms = jnp.mean(acc * acc, axis=-1, keepdims=True)
    hb = (acc * lax.rsqrt(ms + NORM_EPS) * g_ref[...]).astype(BF16)
    mlp = None
    for c in range(0, D_FF, fc):
        u = jnp.dot(hb, wu_ref[:, c:c + fc], preferred_element_type=F32)
        u = jnp.square(jnp.maximum(u, 0.0)).astype(BF16)
        d = jnp.dot(u, wd_ref[c:c + fc, :], preferred_element_type=F32)
        mlp = d if mlp is None else mlp + d
    y = acc + mlp
    if final:
        ms = jnp.mean(y * y, axis=-1, keepdims=True)
        y = y * lax.rsqrt(ms + NORM_EPS) * gf_ref[...]
    out_ref[...] = y


def _out_ffn(stream, lp, outs, wo, g, wu, wd, gf, tm, final):
    first = len(stream) == 2
    B = stream[0].shape[0]
    row_spec = lambda n: pl.BlockSpec((None, tm, n), lambda b, j: (b, j, 0))
    if final:
        rows_out = lp - CHUNK
        specs = [_elem_spec(tm, a.shape[-1], lambda j: CHUNK + j * tm) for a in (stream[0],) + tuple(outs)]
    else:
        rows_out = lp
        specs = _stream_specs(first, tm) + [row_spec(o.shape[-1]) for o in outs]
    return pl.pallas_call(
        functools.partial(_ffn_kernel, final=final, first=first, fc=1024),
        grid=(B, rows_out // tm),
        in_specs=specs + [_resident((D_MODEL, D_MODEL)), _resident((1, D_MODEL)), _resident((D_MODEL, D_FF)),
                          _resident((D_FF, D_MODEL)), _resident((1, D_MODEL))],
        out_specs=row_spec(D_MODEL),
        out_shape=jax.ShapeDtypeStruct((B, rows_out, D_MODEL), F32),
        compiler_params=pltpu.CompilerParams(dimension_semantics=("parallel", "parallel"),
                                             vmem_limit_bytes=VMEM_LIMIT,
                                             allow_input_fusion=[False] * len(specs) + [True, False, True, True, False]),
        name="out_ffn",
    )(*stream, *outs, wo, g, wu, wd, gf)


def _ret_kernel(z_ref, cos_ref, sin_ref, dm_ref, qdec_ref, kdec_t_ref, gcm_ref, o_ref, s_scr):
    bdm = _block_diag_mask()
    bd_ones = bdm.astype(BF16)
    z = z_ref[...]
    q, k, v, g = z[:, 0:256], z[:, 256:512], z[:, 512:768], z[:, 768:1024]
    cos, sin = cos_ref[...], sin_ref[...]
    upper = (_iota((1, GROUP_W), 1) & (HEAD_DIM - 1)) >= HEAD_DIM // 2

    def rot(t):
        swapped = jnp.where(upper, pltpu.roll(t, HEAD_DIM // 2, 1),
                            pltpu.roll(t, GROUP_W - HEAD_DIM // 2, 1))
        return t * cos + swapped * sin

    qr = rot(q)
    kr = rot(k) * HEAD_DIM ** -0.5
    yield
    s = s_scr[...]
    o = _dot(qr * qdec_ref[...], s)
    vb = v.astype(BF16)
    kr_t = kr.T
    head_row = [(_iota((GROUP_W, 1), 0) >> 6) == h for h in range(N_HEADS)]
    head_lane = [(_iota((1, GROUP_W), 1) >> 6) == h for h in range(N_HEADS)]
    kr_tb = kr_t.astype(BF16)
    sc = jnp.dot(qr.astype(BF16), jnp.concatenate([jnp.where(hr, kr_tb, 0.0) for hr in head_row], axis=1),
                 preferred_element_type=F32)
    kv = _dot(kr_t * kdec_t_ref[...], vb)
    s_scr[...] = gcm_ref[...] * s + jnp.where(bdm, kv, 0.0)
    yield
    v_heads = jnp.concatenate([jnp.where(hl, vb, 0.0) for hl in head_lane], axis=0)
    o = o + jnp.dot((sc * dm_ref[...]).astype(BF16), v_heads, preferred_element_type=F32)
    yield
    msq = _head_sum(o * o, bd_ones) * (1.0 / HEAD_DIM)
    o_ref[...] = o * lax.rsqrt(msq + NORM_EPS) * _silu(g)


def _gla_kernel(z_ref, lb_ref, ng_ref, o_ref, s_scr):
    C = CHUNK
    bdm = _block_diag_mask()
    bd_ones = bdm.astype(BF16)
    z = z_ref[...]
    q, f, v, g =z[:, 0:256], z[:, 256:512], z[:, 512:768], z[:, 768:1024]
    lb = lb_ref[...]
    forget = lb + (1.0 - lb) * jax.nn.sigmoid(f)
    k = 1.0 - forget
    logf = jnp.log(forget)
    q = _silu(q) * HEAD_DIM ** -0.5
    row = _iota((C, 1), 0)
    ti, si = _iota((C, C), 0), _iota((C, C), 1)
    b = _cumsum_rows(logf)
    yield

    o = jnp.dot((q * k).astype(BF16), bd_ones, preferred_element_type=F32) * v

    head_lane = [(_iota((1, GROUP_W), 1) >> 6) == h for h in range(N_HEADS)]
    head_row = [(_iota((GROUP_W, 1), 0) >> 6) == h for h in range(N_HEADS)]
    b8 = b.reshape(C // 8, 8, GROUP_W)
    p = None
    m = 1
    while 2 * m <= C:
        if m == 1:
            ref = jnp.where((row & 1) == 1, pltpu.roll(b, 1, 0), b)
        elif m == 2:
            ref = jnp.where(_iota((1, 8, 1), 1) < 4, b8[:, 1:2, :], b8[:, 5:6, :]).reshape(C, GROUP_W)
        else:
            b3 = b.reshape(C // (2 * m), 2 * m, GROUP_W)
            ref = jnp.broadcast_to(b3[:, m - 1:m, :], b3.shape).reshape(C, GROUP_W)
        second = (row & (2 * m - 1)) >= m
        qm = jnp.where(second, q * jnp.exp(b - ref), 0.0).astype(BF16)
        km_t = jnp.where(second, 0.0, k * jnp.exp(ref - b)).T.astype(BF16)
        rk = jnp.concatenate([jnp.where(hr, km_t, 0.0) for hr in head_row], axis=1)
        sc = jnp.dot(qm, rk, preferred_element_type=F32)
        if 2 * m < C:
            sc = jnp.where(jnp.concatenate([(ti ^ si) < 2 * m] * N_HEADS, axis=1), sc, 0.0)
        p = sc if p is None else p + sc
        m *= 2
        yield
    vb = v.astype(BF16)
    v_heads = jnp.concatenate([jnp.where(hl, vb, 0.0) for hl in head_lane], axis=0)
    o = o + jnp.dot(p.astype(BF16), v_heads, preferred_element_type=F32)
    yield

    s = s_scr[...]
    o = o + _dot(q * jnp.exp(b), s)
    b_t = b.T
    b_last = b_t[:, C - 1:C]
    k_end = k.T * jnp.exp(b_last - b_t)
    s_scr[...] = jnp.exp(b_last) * s + jnp.where(bdm, _dot(k_end, vb), 0.0)
    yield
    msq = _head_sum(o * o, bd_ones) * (1.0 / HEAD_DIM)
    o_ref[...] = o * lax.rsqrt(msq + NORM_EPS) * ng_ref[...] * _silu(g)


def _rwkv_kernel(*refs, has_vmix):
    C = CHUNK
    if has_vmix:
        (z_ref, vf_ref, mu_ref, w0_ref, wup_ref, a0_ref, aup_ref, gup_ref, kk_ref, ka_ref, rk_ref,
         lnw_ref, lnb_ref, v0_ref, vdn_ref, vup_ref, o_ref, t_scr, carry_scr) = refs
    else:
        (z_ref, mu_ref, w0_ref, wup_ref, a0_ref, aup_ref, gup_ref, kk_ref, ka_ref, rk_ref,
         lnw_ref, lnb_ref, o_ref, vf_out_ref, t_scr, carry_scr) = refs

    bdm = _block_diag_mask()
    bd_ones = bdm.astype(BF16)
    row = _iota((C, 1), 0)
    ti, si = _iota((C, C), 0), _iota((C, C), 1)

    z = z_ref[...]
    z_prev = jnp.where(row == 0, carry_scr[0:1, :], pltpu.roll(z, 1, 0))
    carry_scr[0:1, :] = z[C - 1:C, :]
    zf = z + (z_prev - z) * mu_ref[...]
    r, k, v = zf[:, 0:256], zf[:, 256:512], zf[:, 512:768]
    wa, gl = zf[:, 768:896], zf[:, 896:RWKV_COLS_PAD]

    w_log = -jax.nn.softplus(-(w0_ref[...] + _dot(jnp.tanh(wa), wup_ref[...]))) - 0.5
    logw = -jnp.exp(w_log)
    a = jax.nn.sigmoid(a0_ref[...] + _dot(wa, aup_ref[...]))
    gate = _dot(jax.nn.sigmoid(gl), gup_ref[...])
    if has_vmix:
        mix = jax.nn.sigmoid(v0_ref[...] + _dot(_dot(v, vdn_ref[...]), vup_ref[...]))
        v = v + (vf_ref[...] - v) * mix
    else:
        vf_out_ref[...] = v
    kk = k * kk_ref[...]
    kk = kk / jnp.maximum(jnp.sqrt(_head_sum(kk * kk, bd_ones)), 1e-12)
    k = k * (1.0 + (a - 1.0) * ka_ref[...])
    alpha = -kk * a
    yield

    gc = _cumsum_rows(logw)
    yield
    gp = gc - logw
    gm = gc[C // 2:C // 2 + 1, :]
    r_mid, b_mid = r * jnp.exp(gc - gm), kk * jnp.exp(gp - gm)
    r_abs, b_abs = r * jnp.exp(gc), kk * jnp.exp(gp)
    g_t, a_t, k_t = gc.T, alpha.T, k.T
    from_mid = jnp.exp(g_t[:, C // 2:C // 2 + 1] - g_t)
    g_last = g_t[:, C - 1:C]
    to_end = jnp.exp(g_last - g_t)

    yield
    head_rows = [(_iota((GROUP_W, 1), 0) >> 6) == h for h in range(N_HEADS)]
    a_mid, k_mid = a_t * from_mid, k_t * from_mid
    rhs_t = jnp.concatenate([jnp.where(m, a_mid, 0.0) for m in head_rows]
                            + [jnp.where(m, k_mid, 0.0) for m in head_rows], axis=1)
    sc = _dot(jnp.concatenate([b_mid, r_mid], axis=0), rhs_t)
    yield

    t0 = t_scr[...]
    t0b = t0.astype(BF16)
    strict2 = jnp.concatenate([ti > si] * 2, axis=1)
    incl2 = jnp.concatenate([ti >= si] * 2, axis=1)
    pairs = range(N_HEADS // 2)
    head_lane = [(_iota((1, GROUP_W), 1) >> 6) == h for h in range(N_HEADS)]
    zeros_cc = jnp.zeros((C, C), BF16)

    def stack(x, p):
        xb = x.astype(BF16)
        return jnp.concatenate([jnp.where(head_lane[2 * p], xb, 0.0),
                                jnp.where(head_lane[2 * p + 1], xb, 0.0)], axis=0)

    def block_diag(xab):
        xb = xab.astype(BF16)
        return jnp.concatenate([jnp.concatenate([xb[:, 0:C], zeros_cc], axis=1),
                                jnp.concatenate([zeros_cc, xb[:, C:2 * C]], axis=1)], axis=0)

    v_stack = [stack(v, p) for p in pairs]
    rhs = jnp.dot(b_abs.astype(BF16), t0b, preferred_element_type=F32)
    for p in pairs:
        bm = jnp.where(strict2, sc[0:C, (N_HEADS + 2 * p) * C:(N_HEADS + 2 * p + 2) * C], 0.0)
        rhs = rhs + jnp.dot(bm.astype(BF16), v_stack[p], preferred_element_type=F32)
    yield

    a_ab = [jnp.where(strict2, sc[0:C, 2 * p * C:(2 * p + 2) * C], 0.0) for p in pairs]
    lower_left = lambda m: jnp.concatenate(
        [((ti & (2 * m - 1)) >= m) & ((si & (2 * m - 1)) < m) & ((ti ^ si) < 2 * m)] * 2, axis=1)
    eye2 = jnp.concatenate([ti == si] * 2, axis=1)
    x_ab = [jnp.where(eye2, 1.0, jnp.where(lower_left(1), a_ab[p], 0.0)) for p in pairs]
    for lvl in range(1, C.bit_length() - 1):
        sel = lower_left(1 << lvl)
        for p in pairs:
            y = jnp.dot(jnp.where(sel, a_ab[p], 0.0).astype(BF16), block_diag(x_ab[p]),
                        preferred_element_type=F32)
            x_ab[p] = x_ab[p] + jnp.dot(x_ab[p].astype(BF16), block_diag(y), preferred_element_type=F32)
        yield
    n_ab = [jnp.where(eye2, 0.0, x_ab[p]) for p in pairs]

    u = rhs + sum(jnp.dot(n_ab[p].astype(BF16), stack(rhs, p), preferred_element_type=F32) for p in pairs)
    yield
    o = jnp.dot(r_abs.astype(BF16), t0b, preferred_element_type=F32)
    for p in pairs:
        ao = jnp.where(incl2, sc[C:2 * C, 2 * p * C:(2 * p + 2) * C], 0.0)
        bo = jnp.where(incl2, sc[C:2 * C, (N_HEADS + 2 * p) * C:(N_HEADS + 2 * p + 2) * C], 0.0)
        o = o + jnp.dot(jnp.concatenate([ao, bo], axis=1).astype(BF16),
                        jnp.concatenate([stack(u, p), v_stack[p]], axis=0), preferred_element_type=F32)
    uv = jnp.concatenate([u, v], axis=0).astype(BF16)
    upd = _dot(jnp.concatenate([a_t * to_end, k_t * to_end], axis=1), uv)
    t_scr[...] = jnp.exp(g_last) * t0 + jnp.where(bdm, upd, 0.0)
    yield

    inv_d = 1.0 / HEAD_DIM
    dev = o - _head_sum(o, bd_ones) * inv_d
    var = _head_sum(dev * dev, bd_ones) * inv_d
    on = dev * lax.rsqrt(var + RWKV_LN_EPS) * lnw_ref[...] + lnb_ref[...]
    on = on + _head_sum(r * k * rk_ref[...], bd_ones) * v
    o_ref[...] = on * gate


def _chunk_mixer(body, rows_per_step, row_args, const_specs, const_args, n_out, scratch, name):
    B, Lp, _ = row_args[0].shape
    rows = _rows_per_step(B, rows_per_step)
    tok = lambda n: pl.BlockSpec((rows, CHUNK, n), lambda b, j: (b, j, 0))
    res = pl.pallas_call(
        _interleaved([(body, len(row_args), len(row_args) + len(const_args), n_out, len(scratch))], rows),
        grid=(B // rows, Lp // CHUNK),
        in_specs=[tok(a.shape[-1]) for a in row_args] + list(const_specs),
        out_specs=[tok(GROUP_W)] * n_out,
        out_shape=[jax.ShapeDtypeStruct((B, Lp, GROUP_W), F32)] * n_out,
        scratch_shapes=[pltpu.VMEM((rows,) + s, F32) for s in scratch],
        compiler_params=pltpu.CompilerParams(dimension_semantics=("parallel", "arbitrary")),
        name=name,
    )(*row_args, *const_args)
    return res


def _retention(z, cos, sin, dm, qdec, kdec_t, gcm):
    pos_spec = pl.BlockSpec((CHUNK, GROUP_W), lambda b, j: (j, 0))
    consts = (dm, qdec, kdec_t, gcm)
    return _chunk_mixer(_ret_kernel, RET_ROWS, [z], [pos_spec, pos_spec] + [_const_spec(a) for a in consts],
                        (cos, sin) + consts, 1, [(GROUP_W, GROUP_W)], "retention")[0]


def _hgrn2(z, lb, ng):
    return _chunk_mixer(_gla_kernel, GLA_ROWS, [z], [_const_spec(lb), _const_spec(ng)], (lb, ng), 1,
                        [(GROUP_W, GROUP_W)], "hgrn2")[0]


def _rwkv7(z, v_first, params, vmix):
    has_vmix = vmix is not None
    consts = tuple(params) + (tuple(vmix) if has_vmix else ())
    res = _chunk_mixer(functools.partial(_rwkv_kernel, has_vmix=has_vmix), RWKV_ROWS,
                       [z, v_first] if has_vmix else [z], [_const_spec(a) for a in consts], consts,
                       1 if has_vmix else 2, [(GROUP_W, GROUP_W), (8, RWKV_COLS_PAD)], "rwkv7")
    return (res[0], v_first) if has_vmix else (res[0], res[1])


def _merge_pieces(sources, src_piece, dst_piece):
    piece = _iota((1, 128), 1) >> 4
    out = None
    for x, s, d in zip(sources, src_piece, dst_piece):
        x = x if s == d else pltpu.roll(x, (16 * (d - s)) % 128, 1)
        out = x if out is None else jnp.where(piece == d, x, out)
    return out


def _s5_kernel(ua_ref, ub_ref, bst_ref, tm_ref, pm_ref, tab_re_ref, tab_im_ref, d_ref, gw_ref, gb_ref,
               oa_ref, ob_ref, c_re_scr, c_im_scr, *, nb):
    sb, n_pairs = S5_BLOCK, S5_GROUPS // 2
    ua = [ua_ref[pl.ds(i, nb, stride=sb), :] for i in range(sb)]
    ub = [ub_ref[pl.ds(i, nb, stride=sb), :] for i in range(sb)]
    tokens = list(range(sb))
    ucat = []
    for q in range(n_pairs):
        src = ua if q < n_pairs // 2 else ub
        ucat.append(jnp.concatenate(
            [_merge_pieces(src, [(2 * q + gg) % 8] * sb, tokens) for gg in range(2)], axis=1).astype(BF16))
    yield
    z64, z128, half = jnp.zeros((128, 64), BF16), jnp.zeros((128, 128), BF16), S5_STATE
    bst, tmm, pmm = [], [], []
    for q in range(n_pairs):
        b0, b1 = bst_ref[2 * q], bst_ref[2 * q + 1]
        bst.append(jnp.concatenate([
            jnp.concatenate([b0[:, 0:half], z64, b0[:, half:], z64], axis=1),
            jnp.concatenate([z64, b1[:, 0:half], z64, b1[:, half:]], axis=1)], axis=0))
        tmm.append(jnp.concatenate([jnp.concatenate([tm_ref[2 * q], z128], axis=1),
                                    jnp.concatenate([z128, tm_ref[2 * q + 1]], axis=1)], axis=0))
        p0, p1 = pm_ref[2 * q], pm_ref[2 * q + 1]
        pmm.append(jnp.concatenate([
            jnp.concatenate([p0[0:half], z128[0:half]], axis=1), jnp.concatenate([z128[0:half], p1[0:half]], axis=1),
            jnp.concatenate([p0[half:], z128[0:half]], axis=1), jnp.concatenate([z128[0:half], p1[half:]], axis=1)],
            axis=0))
    w = [jnp.dot(ucat[q], bst[q], preferred_element_type=F32) for q in range(n_pairs)]
    re = jnp.concatenate([x[:, 0:128] for x in w], axis=1)
    im = jnp.concatenate([x[:, 128:256] for x in w], axis=1)
    yield
    row = _iota((nb, 1), 0)
    first = row == 0
    for sh in (1, 2, 4):
        ar, ai = tab_re_ref[sh - 1:sh, :], tab_im_ref[sh - 1:sh, :]
        inside = (row & 7) >= sh
        sre = jnp.where(inside, pltpu.roll(re, sh, 0), 0.0)
        sim = jnp.where(inside, pltpu.roll(im, sh, 0), 0.0)
        re, im = re + ar * sre - ai * sim, im + ar * sim + ai * sre
    yield
    c0r, c0i = c_re_scr[0:1, :], c_im_scr[0:1, :]
    cr, ci = c0r, c0i
    tr, tq = tab_re_ref[...], tab_im_ref[...]
    groups_re, groups_im = [], []
    for r in range(nb // 8):
        gr = re[8 * r:8 * r + 8] + tr * cr - tq * ci
        gi = im[8 * r:8 * r + 8] + tr * ci + tq * cr
        cr, ci = gr[7:8], gi[7:8]
        groups_re.append(gr)
        groups_im.append(gi)
    re, im = jnp.concatenate(groups_re, axis=0), jnp.concatenate(groups_im, axis=0)
    c_re_scr[0:1, :] = cr
    c_im_scr[0:1, :] = ci
    x_re = jnp.where(first, c0r, pltpu.roll(re, 1, 0)).astype(BF16)
    x_im = jnp.where(first, c0i, pltpu.roll(im, 1, 0)).astype(BF16)
    yield
    y = []
    for q in range(n_pairs):
        x_in = jnp.concatenate([x_re[:, 128 * q:128 * (q + 1)], x_im[:, 128 * q:128 * (q + 1)]], axis=1)
        y.append(jnp.dot(ucat[q], tmm[q], preferred_element_type=F32)
                 + jnp.dot(x_in, pmm[q], preferred_element_type=F32))
    yield
    for i in range(sb):
        halves = []
        for t in range(2):
            groups = range(8 * t, 8 * t + 8)
            halves.append(_merge_pieces([y[g // 2][:, 128 * (g % 2):128 * (g % 2 + 1)] for g in groups],
                                        [i] * 8, [g % 8 for g in groups]))
        yi = jnp.concatenate(halves, axis=1)
        yi = jax.nn.gelu(yi + d_ref[...] * jnp.concatenate([ua[i], ub[i]], axis=1))
        out = yi * jax.nn.sigmoid(_dot(yi, gw_ref[...]) + gb_ref[...])
        oa_ref[pl.ds(i, nb, stride=sb), :] = out[:, 0:128]
        ob_ref[pl.ds(i, nb, stride=sb), :] = out[:, 128:256]
        if i % 2 == 1:
            yield


def _s5(za, zb, bst, tm, pm, tab_re, tab_im, d, gw, gb):
    B, Lp, _ = za.shape
    rows = _rows_per_step(B, S5_ROWS)
    nb = _pick_tile(Lp // S5_BLOCK, (264, 24, 16, 8))
    tk = nb * S5_BLOCK
    tok = pl.BlockSpec((rows, tk, 128), lambda b, j: (b, j, 0))
    small = (tab_re, tab_im, d, gw, gb)
    half = jax.ShapeDtypeStruct((B, Lp, 128), F32)
    return pl.pallas_call(
        _interleaved([(functools.partial(_s5_kernel, nb=nb), 2, 10, 2, 2)], rows),
        grid=(B // rows, Lp // tk),
        in_specs=[tok, tok] + [_const_spec(a) for a in (bst, tm, pm) + small],
        out_specs=[tok, tok],
        out_shape=[half, half],
        scratch_shapes=[pltpu.VMEM((rows, 8, S5_LANES), F32), pltpu.VMEM((rows, 8, S5_LANES), F32)],
        compiler_params=pltpu.CompilerParams(dimension_semantics=("parallel", "arbitrary"),
                                             vmem_limit_bytes=VMEM_LIMIT),
        name="s5",
    )(za, zb, bst, tm, pm, *small)


def _rope_tables(lp):
    half = HEAD_DIM // 2
    pos = jnp.arange(lp) - PAD_FRONT
    inv_freq = ROPE_BASE ** (-jnp.arange(half, dtype=F32) / half)
    ang = pos.astype(F32)[:, None] * inv_freq[None, :]
    cos, sin = jnp.cos(ang), jnp.sin(ang)
    cos_t = jnp.tile(jnp.concatenate([cos, cos], axis=1), (1, N_HEADS))
    sin_t = jnp.tile(jnp.concatenate([-sin, sin], axis=1), (1, N_HEADS))
    return cos_t, sin_t


def _retention_tables():
    C = CHUNK
    log_g = jnp.log1p(-jnp.exp2(-5.0 - jnp.arange(N_HEADS, dtype=F32)))
    t = jnp.arange(C, dtype=F32)
    rel = t[:, None] - t[None, :]
    causal = rel >= 0
    dm = jnp.where(causal[None], jnp.exp(jnp.where(causal, rel, 0.0)[None] * log_g[:, None, None]), 0.0)
    lane_g = jnp.repeat(log_g, HEAD_DIM)
    qdec = jnp.exp((t + 1.0)[:, None] * lane_g[None, :])
    kdec_t = jnp.exp(lane_g[:, None] * (C - 1.0 - t)[None, :])
    gcm = jnp.broadcast_to(jnp.exp(C * lane_g)[:, None], (GROUP_W, GROUP_W))
    dm = jnp.transpose(dm, (1, 0, 2)).reshape(C, N_HEADS * C)
    return dm, qdec, kdec_t, gcm


def _s5_tables(a_re, a_im, log_dt, b_re, b_im, c_re, c_im):
    G, sb = S5_GROUPS, S5_BLOCK
    hi = lax.Precision.HIGHEST
    lam_re, lam_im = a_re.astype(F32), a_im.astype(F32)
    dt = jnp.exp(log_dt.astype(F32))[:, None]
    mag, ph = jnp.exp(lam_re * dt), lam_im * dt
    ab_re, ab_im = mag * jnp.cos(ph), mag * jnp.sin(ph)
    den = lam_re * lam_re + lam_im * lam_im
    nr, ni = ab_re - 1.0, ab_im
    zc_re = (nr * lam_re + ni * lam_im) / den
    zc_im = (ni * lam_re - nr * lam_im) / den
    b_re, b_im = b_re.astype(F32), b_im.astype(F32)
    bb_re = zc_re[..., None] * b_re - zc_im[..., None] * b_im
    bb_im = zc_re[..., None] * b_im + zc_im[..., None] * b_re
    c_re, c_im = c_re.astype(F32), c_im.astype(F32)
    eye = jnp.eye(G, dtype=F32)
    n = jnp.arange(sb + 1, dtype=F32)[:, None, None]
    mag_n, ph_n = jnp.exp(lam_re * dt * n), lam_im * dt * n
    pr, pi = mag_n * jnp.cos(ph_n), mag_n * jnp.sin(ph_n)

    gw = sb * S5_CH

    qr, qi = pr[sb - 1::-1, :, :, None], pi[sb - 1::-1, :, :, None]
    w = jnp.stack([qr * bb_re - qi * bb_im, qr * bb_im + qi * bb_re], axis=0)
    bst = jnp.transpose(w, (2, 1, 4, 0, 3)).reshape(G, gw, 2 * S5_STATE)

    m_re = c_re * pr[:, :, None, :] - c_im * pi[:, :, None, :]
    m_im = c_re * pi[:, :, None, :] + c_im * pr[:, :, None, :]
    coef = jnp.stack([m_re[1:], -m_im[1:]], axis=0)
    pm = jnp.transpose(coef, (2, 0, 4, 1, 3)).reshape(G, 2 * S5_STATE, gw)

    kern = (jnp.einsum('ngcp,gpd->ngcd', m_re[:sb], bb_re, precision=hi)
            - jnp.einsum('ngcp,gpd->ngcd', m_im[:sb], bb_im, precision=hi))
    lag = jnp.arange(sb)[None, :] - jnp.arange(sb)[:, None]
    k_ji = jnp.where((lag >= 0)[:, :, None, None, None], kern[jnp.clip(lag, 0, sb - 1)], 0.0)
    tm = jnp.transpose(k_ji, (2, 0, 4, 1, 3)).reshape(G, gw, gw)

    steps = sb * jnp.arange(1, 9, dtype=F32)[:, None, None]
    m, p = jnp.exp(lam_re * dt * steps), lam_im * dt * steps
    tab_re, tab_im = (m * jnp.cos(p)).reshape(8, S5_LANES), (m * jnp.sin(p)).reshape(8, S5_LANES)
    return bst.astype(BF16), tm.astype(BF16), pm.astype(BF16), tab_re, tab_im


def _pick_tile(lp, candidates):
    for t in candidates:
        if lp % t == 0:
            return t
    return CHUNK


def kernel(x, meta_tokens, norm_mix_g, w_in, hgrn_lb_logits, hgrn_norm_g, rwkv_mu, rwkv_w0, rwkv_w_up,
           rwkv_a0, rwkv_a_up, rwkv_g_up, rwkv_k_k, rwkv_k_a, rwkv_r_k, rwkv_ln_w, rwkv_ln_b, rwkv_v0,
           rwkv_v_down, rwkv_v_up, s5_a_re, s5_a_im, s5_log_dt, s5_b_re, s5_b_im, s5_c_re, s5_c_im, s5_d,
           s5_glu_w, s5_glu_b, w_out, norm_ffn_g, w_ffn_up, w_ffn_down, norm_f_g):
    B, seq, _ = x.shape
    depth = w_in.shape[0]
    assert seq % CHUNK == 0
    lp = CHUNK + seq
    tm = _pick_tile(lp, (704, 384, 256, 128))
    tm_last = _pick_tile(seq, (1024, 512, 256, 128))
    row2 = lambda a: a.astype(F32).reshape(1, -1)

    meta = jnp.broadcast_to(meta_tokens.astype(x.dtype)[None], (B, N_META, D_MODEL))
    head = jnp.concatenate([jnp.zeros((B, PAD_FRONT, D_MODEL), x.dtype), meta], axis=1)
    if depth > 1:
        stream = (head, x)
    else:
        stream = (jnp.concatenate([head, x], axis=1),)

    p = jax.nn.softmax(hgrn_lb_logits.astype(F32), axis=0)
    lower_bounds = jnp.cumsum(p, axis=0) - p[0]
    cos_t, sin_t = _rope_tables(lp)
    dm, qdec, kdec_t, gcm = _retention_tables()

    def pad_rows(a, n):
        return jnp.pad(a, ((0, n - a.shape[0]), (0, 0)))

    w_in_b = w_in.astype(BF16)
    v_first = None
    for l in range(depth):
        z_ret, z_hgrn, z_rwkv, z_s5a, z_s5b = _in_proj(stream, lp, row2(norm_mix_g[l]), w_in_b[l], tm)

        mu = jnp.pad(rwkv_mu[l].astype(F32), (0, RWKV_COLS_PAD - RWKV_COLS)).reshape(1, -1)
        w_up = pad_rows(rwkv_w_up[l], 128).astype(BF16)
        a_up = jnp.concatenate([jnp.zeros_like(rwkv_a_up[l]), rwkv_a_up[l]], axis=0).astype(BF16)
        g_up = pad_rows(rwkv_g_up[l], GROUP_W).astype(BF16)
        params = (mu, row2(rwkv_w0[l]), w_up, row2(rwkv_a0[l]), a_up, g_up, row2(rwkv_k_k[l]),
                  row2(rwkv_k_a[l]), row2(rwkv_r_k[l]), row2(rwkv_ln_w[l]), row2(rwkv_ln_b[l]))
        vmix = None
        if l > 0:
            v_down = jnp.pad(rwkv_v_down[l - 1], ((0, 0), (0, 128 - rwkv_v_down.shape[-1]))).astype(BF16)
            v_up = pad_rows(rwkv_v_up[l - 1], 128).astype(BF16)
            vmix = (row2(rwkv_v0[l - 1]), v_down, v_up)
        o_ret = _retention(z_ret, cos_t, sin_t, dm, qdec, kdec_t, gcm)
        o_hgrn = _hgrn2(z_hgrn, row2(lower_bounds[l]), row2(jnp.tile(hgrn_norm_g[l], N_HEADS)))
        o_rwkv, v_first = _rwkv7(z_rwkv, v_first, params, vmix)

        tabs = _s5_tables(s5_a_re[l], s5_a_im[l], s5_log_dt[l], s5_b_re[l], s5_b_im[l], s5_c_re[l], s5_c_im[l])
        o_s5a, o_s5b = _s5(z_s5a, z_s5b, *tabs, row2(s5_d[l]), s5_glu_w[l].astype(BF16), row2(s5_glu_b[l]))

        last = l == depth - 1
        h = _out_ffn(stream, lp, (o_ret, o_hgrn, o_rwkv, o_s5a, o_s5b), w_out[l].astype(BF16),
                     row2(norm_ffn_g[l]), w_ffn_up[l].astype(BF16), w_ffn_down[l].astype(BF16),
                     row2(norm_f_g), tm_last if last else tm, final=last)
        stream = (h,)
    return h
```
